```python
import jax, jax.numpy as jnp
from jax import lax
import numpy as np

D_MODEL = 1024
BATCH = 8
SEQ = 8192
DEPTH = 4
DEC_BATCH = 32
DEC_SEQ = 2048
PAST_LEN = 128

N_MIXERS = 3
N_A = (DEPTH + 2) // 3
N_B = (DEPTH + 1) // 3
N_C = DEPTH // 3
EPS = 1e-6

A_WIDTH = D_MODEL
A_GROUPS = 8
A_CHUNK = 128
A_GW = A_WIDTH // A_GROUPS

POOL_WINDOWS = (2, 4, 8, 16)
B_GROUPS = len(POOL_WINDOWS)
B_GW = D_MODEL // B_GROUPS

C_HEADS = 8
C_Q_LORA = 768
C_KV_LORA = 256
C_NOPE = 128
C_ROPE = 64
C_V = 128
C_QK = C_NOPE + C_ROPE
C_SCALE = C_QK ** -0.5
ROPE_BASE = 10000.0
Q_BLOCK = 128

D_FF = 2816
CONV_W = 3

kernel_name = "hybrid_bidir_encoder_gmlp_pool_mla"


def rmsnorm(x, g):
    xf = x.astype(jnp.float32)
    y = xf * lax.rsqrt(jnp.mean(xf * xf, axis=-1, keepdims=True) + EPS)
    return (y * g.astype(jnp.float32)).astype(x.dtype)


def rope_tables(S):
    inv = 1.0 / (ROPE_BASE ** (jnp.arange(0, C_ROPE, 2, dtype=jnp.float32) / C_ROPE))
    ang = jnp.arange(S, dtype=jnp.float32)[:, None] * inv[None, :]
    return jnp.cos(ang), jnp.sin(ang)


def apply_rope(x, cos, sin):
    xf = x.astype(jnp.float32)
    half = C_ROPE // 2
    x1, x2 = xf[..., :half], xf[..., half:]
    return jnp.concatenate([x1 * cos - x2 * sin, x1 * sin + x2 * cos], axis=-1).astype(x.dtype)


def mixer_a(h, w_in, b_in, v_norm, w_s, b_s, w_out):
    B, S, _ = h.shape
    z = jax.nn.gelu(h @ w_in + b_in, approximate=False)
    u, v = z[..., :A_WIDTH], z[..., A_WIDTH:]
    v = rmsnorm(v, v_norm)
    vc = v.reshape(B, S // A_CHUNK, A_CHUNK, A_GROUPS, A_GW)
    sv = jnp.einsum("gpq,bnqgc->bnpgc", w_s, vc)
    sv = sv + jnp.transpose(b_s)[None, None, :, :, None]
    return (u * sv.reshape(B, S, A_WIDTH)) @ w_out


def mixer_b(h, w_grp, scale):
    B, S, D = h.shape
    hf = h.astype(jnp.float32)
    cs = jnp.concatenate([jnp.zeros((B, 1, D), jnp.float32), jnp.cumsum(hf, axis=1)], axis=1)
    t = jnp.arange(S)
    outs = []
    for g, w in enumerate(POOL_WINDOWS):
        lo = jnp.clip(t - w // 2, 0, S)
        hi = jnp.clip(t + w // 2, 0, S)
        cnt = (hi - lo).astype(jnp.float32)[None, :, None]
        csg = cs[..., g * B_GW:(g + 1) * B_GW]
        mean = (csg[:, hi] - csg[:, lo]) / cnt
        outs.append(mean - hf[..., g * B_GW:(g + 1) * B_GW])
    pooled = jnp.stack(outs, axis=2).astype(h.dtype)
    y = jnp.einsum("bsgc,gcd->bsgd", pooled, w_grp).reshape(B, S, D)
    return y * scale


def mixer_c(h, w_dq, q_norm, w_uq, w_dkv, kv_norm, w_ukv, w_o):
    B, S, _ = h.shape
    cq = rmsnorm(h @ w_dq, q_norm)
    q = (cq @ w_uq).reshape(B, S, C_HEADS, C_QK)
    q_nope, q_rope = q[..., :C_NOPE], q[..., C_NOPE:]
    kv = h @ w_dkv
    ckv = rmsnorm(kv[..., :C_KV_LORA], kv_norm)
    cos, sin = rope_tables(S)
    q_rope = apply_rope(q_rope, cos[:, None, :], sin[:, None, :])
    k_rope = apply_rope(kv[..., C_KV_LORA:], cos, sin)
    kvu = (ckv @ w_ukv).reshape(B, S, C_HEADS, C_NOPE + C_V)
    k_nope, v = kvu[..., :C_NOPE], kvu[..., C_NOPE:]
    nb = S // Q_BLOCK
    qn = jnp.transpose(q_nope.reshape(B, nb, Q_BLOCK, C_HEADS, C_NOPE), (1, 0, 2, 3, 4))
    qr = jnp.transpose(q_rope.reshape(B, nb, Q_BLOCK, C_HEADS, C_ROPE), (1, 0, 2, 3, 4))

    def attend(args):
        qn_b, qr_b = args
        s = (jnp.einsum("bqhd,bkhd->bhqk", qn_b, k_nope)
             + jnp.einsum("bqhr,bkr->bhqk", qr_b, k_rope))
        p = jax.nn.softmax(s.astype(jnp.float32) * C_SCALE, axis=-1).astype(v.dtype)
        return jnp.einsum("bhqk,bkhd->bqhd", p, v)

    o = lax.map(attend, (qn, qr))
    o = jnp.transpose(o, (1, 0, 2, 3, 4)).reshape(B, S, C_HEADS * C_V)
    return o @ w_o


def channel_mixer(h, w_up, conv_w, conv_b, w_down):
    z = h @ w_up
    zp = jnp.pad(z, ((0, 0), (1, 1), (0, 0)))
    z = zp[:, :-2] * conv_w[0] + zp[:, 1:-1] * conv_w[1] + zp[:, 2:] * conv_w[2] + conv_b
    g, val = z[..., :D_FF], z[..., D_FF:]
    return (jax.nn.silu(g) * val) @ w_down


def encoder_trunk(x, p):
    for i in range(DEPTH):
        j = i // N_MIXERS
        h = rmsnorm(x, p["ln_mix"][i])
        kind = i % N_MIXERS
        if kind == 0:
            x = x + mixer_a(h, p["a_w_in"][j], p["a_b_in"][j], p["a_v_norm"][j],
                            p["a_w_s"][j], p["a_b_s"][j], p["a_w_out"][j])
        elif kind == 1:
            x = x + mixer_b(h, p["b_w_grp"][j], p["b_scale"][j])
        else:
            x = x + mixer_c(h, p["c_w_dq"][j], p["c_q_norm"][j], p["c_w_uq"][j], p["c_w_dkv"][j],
                            p["c_kv_norm"][j], p["c_w_ukv"][j], p["c_w_o"][j])
        h = rmsnorm(x, p["ln_ffn"][i])
        x = x + channel_mixer(h, p["f_w_up"][i], p["f_conv_w"][i], p["f_conv_b"][i], p["f_w_down"][i])
    return rmsnorm(x, p["ln_final"])


def setup_inputs(seed: int = 0) -> dict:
    key = jax.random.key(seed)
    ks = jax.random.split(key, 24)
    f32 = jnp.float32

    def nrm(k, shape, fan_in):
        return jax.random.normal(k, shape, f32) * (fan_in ** -0.5)

    def gain(k, shape):
        return 1.0 + 0.01 * jax.random.normal(k, shape, f32)

    def bias(k, shape):
        return 0.01 * jax.random.normal(k, shape, f32)

    return {
        "x_prompt": jax.random.normal(ks[0], (BATCH, SEQ, D_MODEL), f32),
        "x_sample": jax.random.normal(ks[1], (DEC_BATCH, DEC_SEQ, D_MODEL), f32),
        "ln_mix": gain(ks[2], (DEPTH, D_MODEL)),
        "ln_ffn": gain(ks[3], (DEPTH, D_MODEL)),
        "ln_final": gain(ks[4], (D_MODEL,)),
        "a_w_in": nrm(ks[5], (N_A, D_MODEL, 2 * A_WIDTH), D_MODEL),
        "a_b_in": bias(ks[6], (N_A, 2 * A_WIDTH)),
        "a_v_norm": gain(ks[7], (N_A, A_WIDTH)),
        "a_w_s": nrm(ks[8], (N_A, A_GROUPS, A_CHUNK, A_CHUNK), A_CHUNK),
        "a_b_s": gain(ks[9], (N_A, A_GROUPS, A_CHUNK)),
        "a_w_out": nrm(ks[10], (N_A, A_WIDTH, D_MODEL), A_WIDTH),
        "b_w_grp": nrm(ks[11], (N_B, B_GROUPS, B_GW, B_GW), B_GW),
        "b_scale": gain(ks[12], (N_B, D_MODEL)),
        "c_w_dq": nrm(ks[13], (N_C, D_MODEL, C_Q_LORA), D_MODEL),
        "c_q_norm": gain(ks[14], (N_C, C_Q_LORA)),
        "c_w_uq": nrm(ks[15], (N_C, C_Q_LORA, C_HEADS * C_QK), C_Q_LORA),
        "c_w_dkv": nrm(ks[16], (N_C, D_MODEL, C_KV_LORA + C_ROPE), D_MODEL),
        "c_kv_norm": gain(ks[17], (N_C, C_KV_LORA)),
        "c_w_ukv": nrm(ks[18], (N_C, C_KV_LORA, C_HEADS * (C_NOPE + C_V)), C_KV_LORA),
        "c_w_o": nrm(ks[19], (N_C, C_HEADS * C_V, D_MODEL), C_HEADS * C_V),
        "f_w_up": nrm(ks[20], (DEPTH, D_MODEL, 2 * D_FF), D_MODEL),
        "f_conv_w": nrm(ks[21], (DEPTH, CONV_W, 2 * D_FF), CONV_W),
        "f_conv_b": bias(ks[22], (DEPTH, 2 * D_FF)),
        "f_w_down": nrm(ks[23], (DEPTH, D_FF, D_MODEL), D_FF),
    }


def reference(x_prompt, x_sample, ln_mix, ln_ffn, ln_final, a_w_in, a_b_in, a_v_norm, a_w_s, a_b_s,
              a_w_out, b_w_grp, b_scale, c_w_dq, c_q_norm, c_w_uq, c_w_dkv, c_kv_norm, c_w_ukv, c_w_o,
              f_w_up, f_conv_w, f_conv_b, f_w_down):
    p = {
        "ln_mix": ln_mix, "ln_ffn": ln_ffn, "ln_final": ln_final,
        "a_w_in": a_w_in, "a_b_in": a_b_in, "a_v_norm": a_v_norm, "a_w_s": a_w_s, "a_b_s": a_b_s,
        "a_w_out": a_w_out,
        "b_w_grp": b_w_grp, "b_scale": b_scale,
        "c_w_dq": c_w_dq, "c_q_norm": c_q_norm, "c_w_uq": c_w_uq, "c_w_dkv": c_w_dkv,
        "c_kv_norm": c_kv_norm, "c_w_ukv": c_w_ukv, "c_w_o": c_w_o,
        "f_w_up": f_w_up, "f_conv_w": f_conv_w, "f_conv_b": f_conv_b, "f_w_down": f_w_down,
    }
    y_prompt = encoder_trunk(x_prompt, p)
    y_sample = encoder_trunk(x_sample, p)
    return (y_prompt, y_sample)
```

```python
import functools
import math

import jax
import jax.numpy as jnp
import numpy as np
from jax import lax
from jax.experimental import pallas as pl
from jax.experimental.pallas import tpu as pltpu

F32 = jnp.float32
BF16 = jnp.bfloat16

EPS = 1e-6
A_GROUPS = 8
A_CHUNK = 128
POOL_WINDOWS = (2, 4, 8, 16)
C_HEADS = 8
C_NOPE = 128
C_ROPE = 64
C_V = 128
C_KV_LORA = 256
C_QK = C_NOPE + C_ROPE
ROPE_BASE = 10000.0

LANES = 128
SUBLANES = 8
HALO = SUBLANES
HEAD_PAD = 2 * LANES
VMEM_LIMIT = 56 * 1024 * 1024

TM = 512
FF_CHUNK = 256
TQ = 512
TK = 512


def _dot(a, b):
    return jnp.dot(a, b, preferred_element_type=F32)


def _rms(x, g):
    return x * lax.rsqrt(jnp.mean(x * x, axis=-1, keepdims=True) + EPS) * g


def _params(n_axes=1):
    return pltpu.CompilerParams(dimension_semantics=("arbitrary",) * n_axes, vmem_limit_bytes=VMEM_LIMIT)


def _resident(shape):
    nd = len(shape)
    return pl.BlockSpec(shape, lambda *_: (0,) * nd, pipeline_mode=pl.Buffered(1))


def _halo_specs(tm, d, n_tiles):
    r = tm // HALO
    last = n_tiles * r - 1
    prev = pl.BlockSpec((HALO, d), lambda i: (jnp.maximum(i * r - 1, 0), 0))
    main = pl.BlockSpec((tm, d), lambda i: (i, 0))
    nxt = pl.BlockSpec((HALO, d), lambda i: (jnp.minimum((i + 1) * r, last), 0))
    return prev, main, nxt


def _ffn_kernel(xp_ref, x_ref, xn_ref, g_ref, wup_ref, cw_ref, cb_ref, wdn_ref, gf_ref, o_ref, *,
                tiles_per_seq, n_chunks, final_norm):
    tm = x_ref.shape[0]
    j = pl.program_id(0) % tiles_per_seq
    keep_prev = (j != 0).astype(F32)
    keep_next = (j != tiles_per_seq - 1).astype(F32)
    g = g_ref[...]
    x = x_ref[...]
    hext = jnp.concatenate([_rms(xp_ref[...], g) * keep_prev, _rms(x, g), _rms(xn_ref[...], g) * keep_next],
                           axis=0).astype(BF16)
    rows = tm + 2 * HALO

    def conv(z, half, c):
        cw = cw_ref[half, c]
        zp = pltpu.roll(z, 1, 0)[HALO:HALO + tm]
        zn = pltpu.roll(z, rows - 1, 0)[HALO:HALO + tm]
        return zp * cw[0:1] + z[HALO:HALO + tm] * cw[1:2] + zn * cw[2:3] + cb_ref[half, c]

    acc = jnp.zeros((tm, x.shape[1]), F32)
    for c in range(n_chunks):
        gate = conv(_dot(hext, wup_ref[0, c]), 0, c)
        val = conv(_dot(hext, wup_ref[1, c]), 1, c)
        act = (gate * jax.nn.sigmoid(gate) * val).astype(BF16)
        acc = acc + _dot(act, wdn_ref[c])
    y = x + acc
    if final_norm:
        y = _rms(y, gf_ref[...])
    o_ref[...] = y


def _ffn(x, ln, w_up, conv_w, conv_b, w_down, ln_final, seq, final_norm):
    t, d = x.shape
    f = w_down.shape[0]
    n_chunks = f // FF_CHUNK
    n_tiles = t // TM
    wup = w_up.astype(BF16).reshape(d, 2, n_chunks, FF_CHUNK).transpose(1, 2, 0, 3)
    cw = conv_w.reshape(3, 2, n_chunks, FF_CHUNK).transpose(1, 2, 0, 3)
    cb = conv_b.reshape(2, n_chunks, 1, FF_CHUNK)
    wdn = w_down.astype(BF16).reshape(n_chunks, FF_CHUNK, d)
    prev, main, nxt = _halo_specs(TM, d, n_tiles)
    kern = functools.partial(_ffn_kernel, tiles_per_seq=seq // TM, n_chunks=n_chunks, final_norm=final_norm)
    return pl.pallas_call(
        kern,
        grid=(n_tiles,),
        in_specs=[prev, main, nxt, _resident((1, d)), _resident(wup.shape), _resident(cw.shape),
                  _resident(cb.shape), _resident(wdn.shape), _resident((1, d))],
        out_specs=pl.BlockSpec((TM, d), lambda i: (i, 0)),
        out_shape=jax.ShapeDtypeStruct((t, d), F32),
        compiler_params=_params(),
        name="ffn",
    )(x, x, x, ln.reshape(1, d), wup, cw, cb, wdn, ln_final.reshape(1, d))


def _mixer_a_kernel(x_ref, g_ref, win_ref, bin_ref, vn_ref, ws_ref, bs_ref, wout_ref, o_ref, sv_ref):
    tm = x_ref.shape[0]
    width = wout_ref.shape[0]
    gw = width // A_GROUPS
    x = x_ref[...]
    h = _rms(x, g_ref[...]).astype(BF16)
    z = _dot(h, win_ref[...]) + bin_ref[...]
    z = 0.5 * z * (1.0 + lax.erf(z * (1.0 / math.sqrt(2.0))))
    u = z[:, :width]
    v = _rms(z[:, width:], vn_ref[...]).astype(BF16)
    for n in range(tm // A_CHUNK):
        r0 = n * A_CHUNK
        for a in range(A_GROUPS):
            c0 = a * gw
            sv_ref[r0:r0 + A_CHUNK, c0:c0 + gw] = _dot(ws_ref[a], v[r0:r0 + A_CHUNK, c0:c0 + gw]) + bs_ref[a]
    o_ref[...] = x + _dot((u * sv_ref[...]).astype(BF16), wout_ref[...])


def _mixer_a(x, ln, w_in, b_in, v_norm, w_s, b_s, w_out):
    t, d = x.shape
    width = w_out.shape[0]
    gw = width // A_GROUPS
    bsb = jnp.broadcast_to(b_s[:, :, None], (A_GROUPS, A_CHUNK, gw))
    return pl.pallas_call(
        _mixer_a_kernel,
        grid=(t // TM,),
        in_specs=[pl.BlockSpec((TM, d), lambda i: (i, 0)), _resident((1, d)), _resident(w_in.shape),
                  _resident((1, 2 * width)), _resident((1, width)), _resident(w_s.shape), _resident(bsb.shape),
                  _resident(w_out.shape)],
        out_specs=pl.BlockSpec((TM, d), lambda i: (i, 0)),
        out_shape=jax.ShapeDtypeStruct((t, d), F32),
        scratch_shapes=[pltpu.VMEM((TM, width), F32)],
        compiler_params=_params(),
        name="mixer_a",
    )(x, ln.reshape(1, d), w_in.astype(BF16), b_in.reshape(1, -1), v_norm.reshape(1, -1), w_s.astype(BF16), bsb,
      w_out.astype(BF16))


def _mixer_b_kernel(xp_ref, x_ref, xn_ref, g_ref, pool_ref, wg_ref, sc_ref, o_ref, *, tiles_per_seq, seq):
    tm, d = x_ref.shape
    gw = d // len(POOL_WINDOWS)
    j = pl.program_id(0) % tiles_per_seq
    keep_prev = (j != 0).astype(F32)
    keep_next = (j != tiles_per_seq - 1).astype(F32)
    g = g_ref[...]
    x = x_ref[...]
    h = _rms(x, g)
    hext = jnp.concatenate([_rms(xp_ref[...], g) * keep_prev, h, _rms(xn_ref[...], g) * keep_next], axis=0)
    hi = hext.astype(BF16)
    lo = (hext - hi.astype(F32)).astype(BF16)
    pos = j * tm + lax.broadcasted_iota(jnp.int32, (tm, 1), 0)
    for a, w in enumerate(POOL_WINDOWS):
        c0 = a * gw
        band = pool_ref[a]
        wsum = _dot(band, hi[:, c0:c0 + gw]) + _dot(band, lo[:, c0:c0 + gw])
        cnt = (jnp.minimum(pos + w // 2, seq) - jnp.maximum(pos - w // 2, 0)).astype(F32)
        pooled = (wsum / cnt - h[:, c0:c0 + gw]).astype(BF16)
        o_ref[:, c0:c0 + gw] = x[:, c0:c0 + gw] + _dot(pooled, wg_ref[a]) * sc_ref[:, c0:c0 + gw]


def _pool_bands(tm):
    t = np.arange(tm)[:, None] + HALO
    jj = np.arange(tm + 2 * HALO)[None, :]
    return np.stack([((jj >= t - w // 2) & (jj < t + w // 2)) for w in POOL_WINDOWS]).astype(np.float32)


def _mixer_b(x, ln, w_grp, scale, seq):
    t, d = x.shape
    n_tiles = t // TM
    bands = jnp.asarray(_pool_bands(TM), BF16)
    prev, main, nxt = _halo_specs(TM, d, n_tiles)
    kern = functools.partial(_mixer_b_kernel, tiles_per_seq=seq // TM, seq=seq)
    return pl.pallas_call(
        kern,
        grid=(n_tiles,),
        in_specs=[prev, main, nxt, _resident((1, d)), _resident(bands.shape), _resident(w_grp.shape),
                  _resident((1, d))],
        out_specs=pl.BlockSpec((TM, d), lambda i: (i, 0)),
        out_shape=jax.ShapeDtypeStruct((t, d), F32),
        compiler_params=_params(),
        name="mixer_b",
    )(x, x, x, ln.reshape(1, d), bands, w_grp.astype(BF16), scale.reshape(1, d))


def _rope(r, cos, sin_lo, sin_hi):
    half = C_ROPE // 2
    return r * cos + pltpu.roll(r, LANES - half, 1) * sin_lo + pltpu.roll(r, half, 1) * sin_hi


def _mla_proj_kernel(x_ref, g_ref, wdq_ref, qn_ref, wuq_ref, wdkv_ref, kvn_ref, wukv_ref, cos_ref, slo_ref, shi_ref,
                     q_ref, k_ref, v_ref, *, q_scale):
    h = _rms(x_ref[...], g_ref[...]).astype(BF16)
    cos, slo, shi = cos_ref[...], slo_ref[...], shi_ref[...]
    cq = _rms(_dot(h, wdq_ref[...]), qn_ref[...]).astype(BF16)
    q = _dot(cq, wuq_ref[...])
    kv = _dot(h, wdkv_ref[...])
    ckv = _rms(kv[:, :C_KV_LORA], kvn_ref[...]).astype(BF16)
    k_rope = _rope(kv[:, C_KV_LORA:], cos, slo, shi).astype(BF16)
    kvu = _dot(ckv, wukv_ref[...])
    n_k = C_HEADS * C_NOPE
    for a in range(C_HEADS):
        q0 = a * HEAD_PAD
        q_ref[:, q0:q0 + C_NOPE] = (q[:, q0:q0 + C_NOPE] * q_scale).astype(BF16)
        q_ref[:, q0 + C_NOPE:q0 + HEAD_PAD] = (_rope(q[:, q0 + C_NOPE:q0 + HEAD_PAD], cos, slo, shi)
                                               * q_scale).astype(BF16)
        k_ref[:, q0:q0 + C_NOPE] = kvu[:, a * C_NOPE:(a + 1) * C_NOPE].astype(BF16)
        k_ref[:, q0 + C_NOPE:q0 + HEAD_PAD] = k_rope
    v_ref[...] = kvu[:, n_k:].astype(BF16)


def _rope_lane_tables(seq):
    half = C_ROPE // 2
    inv = 1.0 / (ROPE_BASE ** (jnp.arange(0, C_ROPE, 2, dtype=F32) / C_ROPE))
    ang = jnp.arange(seq, dtype=F32)[:, None] * inv[None, :]
    cos, sin = jnp.cos(ang), jnp.sin(ang)
    zero = jnp.zeros((seq, half), F32)
    pad = jnp.zeros((seq, LANES - C_ROPE), F32)
    cos_t = jnp.concatenate([cos, cos, pad], axis=1)
    sin_lo = jnp.concatenate([-sin, zero, pad], axis=1)
    sin_hi = jnp.concatenate([zero, sin, pad], axis=1)
    return cos_t, sin_lo, sin_hi


def _mla_proj(x, ln, w_dq, q_norm, w_uq, w_dkv, kv_norm, w_ukv, seq):
    t, d = x.shape
    q_lora = w_dq.shape[1]
    wuq = w_uq.reshape(q_lora, C_HEADS, C_QK)
    wuq = jnp.pad(wuq, ((0, 0), (0, 0), (0, HEAD_PAD - C_QK))).reshape(q_lora, C_HEADS * HEAD_PAD).astype(BF16)
    wdkv = jnp.pad(w_dkv, ((0, 0), (0, LANES - C_ROPE))).astype(BF16)
    wukv = w_ukv.reshape(C_KV_LORA, C_HEADS, C_NOPE + C_V)
    wukv = jnp.concatenate([wukv[:, :, :C_NOPE].reshape(C_KV_LORA, -1), wukv[:, :, C_NOPE:].reshape(C_KV_LORA, -1)],
                           axis=1).astype(BF16)
    cos_t, sin_lo, sin_hi = _rope_lane_tables(seq)
    tps = seq // TM
    tab = pl.BlockSpec((TM, LANES), lambda i: (i % tps, 0))
    tok = lambda w: pl.BlockSpec((TM, w), lambda i: (i, 0))
    kern = functools.partial(_mla_proj_kernel, q_scale=(C_QK ** -0.5) * math.log2(math.e))
    return pl.pallas_call(
        kern,
        grid=(t // TM,),
        in_specs=[tok(d), _resident((1, d)), _resident(w_dq.shape), _resident((1, q_lora)), _resident(wuq.shape),
                  _resident(wdkv.shape), _resident((1, C_KV_LORA)), _resident(wukv.shape), tab, tab, tab],
        out_specs=[tok(C_HEADS * HEAD_PAD), tok(C_HEADS * HEAD_PAD), tok(C_HEADS * C_V)],
        out_shape=[jax.ShapeDtypeStruct((t, C_HEADS * HEAD_PAD), BF16),
                   jax.ShapeDtypeStruct((t, C_HEADS * HEAD_PAD), BF16),
                   jax.ShapeDtypeStruct((t, C_HEADS * C_V), BF16)],
        compiler_params=_params(),
        name="mla_proj",
    )(x, ln.reshape(1, d), w_dq.astype(BF16), q_norm.reshape(1, -1), wuq, wdkv, kv_norm.reshape(1, -1), wukv,
      cos_t, sin_lo, sin_hi)


def _attn_kernel(q_ref, k_ref, v_ref, o_ref, m_ref, l_ref, acc_ref, *, n_kv):
    q = q_ref[...]
    m_ref[...] = jnp.full(m_ref.shape, -jnp.inf, F32)
    l_ref[...] = jnp.zeros(l_ref.shape, F32)
    acc_ref[...] = jnp.zeros(acc_ref.shape, F32)

    def step(j, carry):
        r0 = pl.multiple_of(j * TK, TK)
        s = lax.dot_general(q, k_ref[pl.ds(r0, TK), :], (((1,), (1,)), ((), ())), preferred_element_type=F32)
        m_old = m_ref[...]
        m_new = jnp.maximum(m_old, jnp.max(s, axis=-1, keepdims=True))
        p = jnp.exp2(s - m_new)
        alpha = jnp.exp2(m_old - m_new)
        l_ref[...] = alpha * l_ref[...] + jnp.sum(p, axis=-1, keepdims=True)
        acc_ref[...] = alpha * acc_ref[...] + _dot(p.astype(BF16), v_ref[pl.ds(r0, TK), :])
        m_ref[...] = m_new
        return carry

    lax.fori_loop(0, n_kv, step, 0)
    o_ref[...] = (acc_ref[...] / l_ref[...]).astype(o_ref.dtype)


def _attention(q, k, v, batch, seq):
    t = q.shape[0]
    nq = seq // TQ
    kern = functools.partial(_attn_kernel, n_kv=seq // TK)
    return pl.pallas_call(
        kern,
        grid=(batch, C_HEADS, nq),
        in_specs=[pl.BlockSpec((TQ, HEAD_PAD), lambda b, a, i: (b * nq + i, a)),
                  pl.BlockSpec((seq, HEAD_PAD), lambda b, a, i: (b, a)),
                  pl.BlockSpec((seq, C_V), lambda b, a, i: (b, a))],
        out_specs=pl.BlockSpec((TQ, C_V), lambda b, a, i: (b * nq + i, a)),
        out_shape=jax.ShapeDtypeStruct((t, C_HEADS * C_V), BF16),
        scratch_shapes=[pltpu.VMEM((TQ, 1), F32), pltpu.VMEM((TQ, 1), F32), pltpu.VMEM((TQ, C_V), F32)],
        compiler_params=_params(3),
        name="mla_attn",
    )(q, k, v)


def _out_proj_kernel(x_ref, o_ref, w_ref, y_ref):
    y_ref[...] = x_ref[...] + _dot(o_ref[...], w_ref[...])


def _out_proj(x, o, w_o):
    t, d = x.shape
    return pl.pallas_call(
        _out_proj_kernel,
        grid=(t // TM,),
        in_specs=[pl.BlockSpec((TM, d), lambda i: (i, 0)), pl.BlockSpec((TM, o.shape[1]), lambda i: (i, 0)),
                  _resident(w_o.shape)],
        out_specs=pl.BlockSpec((TM, d), lambda i: (i, 0)),
        out_shape=jax.ShapeDtypeStruct((t, d), F32),
        compiler_params=_params(),
        name="mla_out",
    )(x, o, w_o.astype(BF16))


def _trunk(x3, p):
    batch, seq, d = x3.shape
    assert seq % TM == 0 and seq % TQ == 0 and seq % TK == 0 and TM % A_CHUNK == 0
    x = x3.reshape(batch * seq, d)
    depth = p["ln_mix"].shape[0]
    for i in range(depth):
        j, kind = divmod(i, 3)
        if kind == 0:
            x = _mixer_a(x, p["ln_mix"][i], p["a_w_in"][j], p["a_b_in"][j], p["a_v_norm"][j], p["a_w_s"][j],
                         p["a_b_s"][j], p["a_w_out"][j])
        elif kind == 1:
            x = _mixer_b(x, p["ln_mix"][i], p["b_w_grp"][j], p["b_scale"][j], seq)
        else:
            q, k, v = _mla_proj(x, p["ln_mix"][i], p["c_w_dq"][j], p["c_q_norm"][j], p["c_w_uq"][j],
                                p["c_w_dkv"][j], p["c_kv_norm"][j], p["c_w_ukv"][j], seq)
            x = _out_proj(x, _attention(q, k, v, batch, seq), p["c_w_o"][j])
        x = _ffn(x, p["ln_ffn"][i], p["f_w_up"][i], p["f_conv_w"][i], p["f_conv_b"][i], p["f_w_down"][i],
                 p["ln_final"], seq, final_norm=(i == depth - 1))
    return x.reshape(batch, seq, d)


def kernel(x_prompt, x_sample, ln_mix, ln_ffn, ln_final, a_w_in, a_b_in, a_v_norm, a_w_s, a_b_s, a_w_out, b_w_grp,
           b_scale, c_w_dq, c_q_norm, c_w_uq, c_w_dkv, c_kv_norm, c_w_ukv, c_w_o, f_w_up, f_conv_w, f_conv_b,
           f_w_down):
    p = dict(ln_mix=ln_mix, ln_ffn=ln_ffn, ln_final=ln_final, a_w_in=a_w_in, a_b_in=a_b_in, a_v_norm=a_v_norm,
             a_w_s=a_w_s, a_b_s=a_b_s, a_w_out=a_w_out, b_w_grp=b_w_grp, b_scale=b_scale, c_w_dq=c_w_dq,
             c_q_norm=c_q_norm, c_w_uq=c_w_uq, c_w_dkv=c_w_dkv, c_kv_norm=c_kv_norm, c_w_ukv=c_w_ukv, c_w_o=c_w_o,
             f_w_up=f_w_up, f_conv_w=f_conv_w, f_conv_b=f_conv_b, f_w_down=f_w_down)
    return (_trunk(x_prompt, p), _trunk(x_sample, p))
```

```python
import functools
import math

import jax
import jax.numpy as jnp
import numpy as np
from jax import lax
from jax.experimental import pallas as pl
from jax.experimental.pallas import tpu as pltpu

F32 = jnp.float32
BF16 = jnp.bfloat16

EPS = 1e-6
A_GROUPS = 8
A_CHUNK = 128
POOL_WINDOWS = (2, 4, 8, 16)
C_HEADS = 8
C_NOPE = 128
C_ROPE = 64
C_V = 128
C_KV_LORA = 256
C_QK = C_NOPE + C_ROPE
ROPE_BASE = 10000.0

LANES = 128
SUBLANES = 8
HALO = SUBLANES
HEAD_PAD = 2 * LANES
VMEM_LIMIT = 56 * 1024 * 1024

TM = 512
FF_CHUNK = 256
TQ = 512
TK = 512
HEADS_PER_STEP = 2


def _dot(a, b):
    return jnp.dot(a, b, preferred_element_type=F32)


def _rms(x, g):
    return x * lax.rsqrt(jnp.mean(x * x, axis=-1, keepdims=True) + EPS) * g


def _params(n_axes=1):
    return pltpu.CompilerParams(dimension_semantics=("arbitrary",) * n_axes, vmem_limit_bytes=VMEM_LIMIT)


def _resident(shape):
    nd = len(shape)
    return pl.BlockSpec(shape, lambda *_: (0,) * nd, pipeline_mode=pl.Buffered(1))


def _halo_specs(tm, d, n_tiles):
    r = tm // HALO
    last = n_tiles * r - 1
    prev = pl.BlockSpec((HALO, d), lambda i: (jnp.maximum(i * r - 1, 0), 0))
    main = pl.BlockSpec((tm, d), lambda i: (i, 0))
    nxt = pl.BlockSpec((HALO, d), lambda i: (jnp.minimum((i + 1) * r, last), 0))
    return prev, main, nxt


def _ffn_kernel(xp_ref, x_ref, xn_ref, g_ref, wup_ref, cw_ref, cb_ref, wdn_ref, gf_ref, o_ref, *,
                tiles_per_seq, n_chunks, final_norm):
    tm = x_ref.shape[0]
    j = pl.program_id(0) % tiles_per_seq
    keep_prev = (j != 0).astype(F32)
    keep_next = (j != tiles_per_seq - 1).astype(F32)
    g = g_ref[...]
    x = x_ref[...]
    hext = jnp.concatenate([_rms(xp_ref[...], g) * keep_prev, _rms(x, g), _rms(xn_ref[...], g) * keep_next],
                           axis=0).astype(BF16)
    rows = tm + 2 * HALO

    def conv(z, half, c):
        cw = cw_ref[half, c]
        zp = pltpu.roll(z, 1, 0)[HALO:HALO + tm]
        zn = pltpu.roll(z, rows - 1, 0)[HALO:HALO + tm]
        return zp * cw[0:1] + z[HALO:HALO + tm] * cw[1:2] + zn * cw[2:3] + cb_ref[half, c]

    acc = jnp.zeros((tm, x.shape[1]), F32)
    for c in range(n_chunks):
        gate = conv(_dot(hext, wup_ref[0, c]), 0, c)
        val = conv(_dot(hext, wup_ref[1, c]), 1, c)
        act = (gate * jax.nn.sigmoid(gate) * val).astype(BF16)
        acc = acc + _dot(act, wdn_ref[c])
    y = x + acc
    if final_norm:
        y = _rms(y, gf_ref[...])
    o_ref[...] = y


def _ffn(x, ln, w_up, conv_w, conv_b, w_down, ln_final, seq, final_norm):
    t, d = x.shape
    f = w_down.shape[0]
    n_chunks = f // FF_CHUNK
    n_tiles = t // TM
    wup = w_up.astype(BF16).reshape(d, 2, n_chunks, FF_CHUNK).transpose(1, 2, 0, 3)
    cw = conv_w.reshape(3, 2, n_chunks, FF_CHUNK).transpose(1, 2, 0, 3)
    cb = conv_b.reshape(2, n_chunks, 1, FF_CHUNK)
    wdn = w_down.astype(BF16).reshape(n_chunks, FF_CHUNK, d)
    prev, main, nxt = _halo_specs(TM, d, n_tiles)
    kern = functools.partial(_ffn_kernel, tiles_per_seq=seq // TM, n_chunks=n_chunks, final_norm=final_norm)
    return pl.pallas_call(
        kern,
        grid=(n_tiles,),
        in_specs=[prev, main, nxt, _resident((1, d)), _resident(wup.shape), _resident(cw.shape),
                  _resident(cb.shape), _resident(wdn.shape), _resident((1, d))],
        out_specs=pl.BlockSpec((TM, d), lambda i: (i, 0)),
        out_shape=jax.ShapeDtypeStruct((t, d), F32),
        compiler_params=_params(),
        name="ffn",
    )(x, x, x, ln.reshape(1, d), wup, cw, cb, wdn, ln_final.reshape(1, d))


def _mixer_a_kernel(x_ref, g_ref, win_ref, bin_ref, vn_ref, ws_ref, bs_ref, wout_ref, o_ref, sv_ref):
    tm = x_ref.shape[0]
    width = wout_ref.shape[0]
    gw = width // A_GROUPS
    x = x_ref[...]
    h = _rms(x, g_ref[...]).astype(BF16)
    z = _dot(h, win_ref[...]) + bin_ref[...]
    z = 0.5 * z * (1.0 + lax.erf(z * (1.0 / math.sqrt(2.0))))
    u = z[:, :width]
    v = _rms(z[:, width:], vn_ref[...]).astype(BF16)
    for n in range(tm // A_CHUNK):
        r0 = n * A_CHUNK
        for a in range(A_GROUPS):
            c0 = a * gw
            sv_ref[r0:r0 + A_CHUNK, c0:c0 + gw] = _dot(ws_ref[a], v[r0:r0 + A_CHUNK, c0:c0 + gw]) + bs_ref[a]
    o_ref[...] = x + _dot((u * sv_ref[...]).astype(BF16), wout_ref[...])


def _mixer_a(x, ln, w_in, b_in, v_norm, w_s, b_s, w_out):
    t, d = x.shape
    width = w_out.shape[0]
    gw = width // A_GROUPS
    bsb = jnp.broadcast_to(b_s[:, :, None], (A_GROUPS, A_CHUNK, gw))
    return pl.pallas_call(
        _mixer_a_kernel,
        grid=(t // TM,),
        in_specs=[pl.BlockSpec((TM, d), lambda i: (i, 0)), _resident((1, d)), _resident(w_in.shape),
                  _resident((1, 2 * width)), _resident((1, width)), _resident(w_s.shape), _resident(bsb.shape),
                  _resident(w_out.shape)],
        out_specs=pl.BlockSpec((TM, d), lambda i: (i, 0)),
        out_shape=jax.ShapeDtypeStruct((t, d), F32),
        scratch_shapes=[pltpu.VMEM((TM, width), F32)],
        compiler_params=_params(),
        name="mixer_a",
    )(x, ln.reshape(1, d), w_in.astype(BF16), b_in.reshape(1, -1), v_norm.reshape(1, -1), w_s.astype(BF16), bsb,
      w_out.astype(BF16))


def _mixer_b_kernel(xp_ref, x_ref, xn_ref, g_ref, pool_ref, wg_ref, sc_ref, o_ref, *, tiles_per_seq, seq):
    tm, d = x_ref.shape
    gw = d // len(POOL_WINDOWS)
    j = pl.program_id(0) % tiles_per_seq
    keep_prev = (j != 0).astype(F32)
    keep_next = (j != tiles_per_seq - 1).astype(F32)
    g = g_ref[...]
    x = x_ref[...]
    h = _rms(x, g)
    hext = jnp.concatenate([_rms(xp_ref[...], g) * keep_prev, h, _rms(xn_ref[...], g) * keep_next], axis=0)
    hi = hext.astype(BF16)
    lo = (hext - hi.astype(F32)).astype(BF16)
    pos = j * tm + lax.broadcasted_iota(jnp.int32, (tm, 1), 0)
    for a, w in enumerate(POOL_WINDOWS):
        c0 = a * gw
        band = pool_ref[a]
        wsum = _dot(band, hi[:, c0:c0 + gw]) + _dot(band, lo[:, c0:c0 + gw])
        cnt = (jnp.minimum(pos + w // 2, seq) - jnp.maximum(pos - w // 2, 0)).astype(F32)
        pooled = (wsum / cnt - h[:, c0:c0 + gw]).astype(BF16)
        o_ref[:, c0:c0 + gw] = x[:, c0:c0 + gw] + _dot(pooled, wg_ref[a]) * sc_ref[:, c0:c0 + gw]


def _pool_bands(tm):
    t = np.arange(tm)[:, None] + HALO
    jj = np.arange(tm + 2 * HALO)[None, :]
    return np.stack([((jj >= t - w // 2) & (jj < t + w // 2)) for w in POOL_WINDOWS]).astype(np.float32)


def _mixer_b(x, ln, w_grp, scale, seq):
    t, d = x.shape
    n_tiles = t // TM
    bands = jnp.asarray(_pool_bands(TM), BF16)
    prev, main, nxt = _halo_specs(TM, d, n_tiles)
    kern = functools.partial(_mixer_b_kernel, tiles_per_seq=seq // TM, seq=seq)
    return pl.pallas_call(
        kern,
        grid=(n_tiles,),
        in_specs=[prev, main, nxt, _resident((1, d)), _resident(bands.shape), _resident(w_grp.shape),
                  _resident((1, d))],
        out_specs=pl.BlockSpec((TM, d), lambda i: (i, 0)),
        out_shape=jax.ShapeDtypeStruct((t, d), F32),
        compiler_params=_params(),
        name="mixer_b",
    )(x, x, x, ln.reshape(1, d), bands, w_grp.astype(BF16), scale.reshape(1, d))


def _rope(r, cos, sin_lo, sin_hi):
    half = C_ROPE // 2
    return r * cos + pltpu.roll(r, LANES - half, 1) * sin_lo + pltpu.roll(r, half, 1) * sin_hi


def _mla_proj_kernel(x_ref, g_ref, wdq_ref, qn_ref, wuq_ref, wdkv_ref, kvn_ref, wuk_ref, wvt_ref, cos_ref, slo_ref,
                     shi_ref, q_ref, k_ref, vt_ref, *, q_scale):
    h = _rms(x_ref[...], g_ref[...]).astype(BF16)
    cos, slo, shi = cos_ref[...], slo_ref[...], shi_ref[...]
    cq = _rms(_dot(h, wdq_ref[...]), qn_ref[...]).astype(BF16)
    q = _dot(cq, wuq_ref[...])
    kv = _dot(h, wdkv_ref[...])
    ckv = _rms(kv[:, :C_KV_LORA], kvn_ref[...]).astype(BF16)
    k_rope = _rope(kv[:, C_KV_LORA:], cos, slo, shi).astype(BF16)
    k_nope = _dot(ckv, wuk_ref[...])
    for a in range(C_HEADS):
        q0 = a * HEAD_PAD
        q_ref[:, q0:q0 + C_NOPE] = (q[:, q0:q0 + C_NOPE] * q_scale).astype(BF16)
        q_ref[:, q0 + C_NOPE:q0 + HEAD_PAD] = (_rope(q[:, q0 + C_NOPE:q0 + HEAD_PAD], cos, slo, shi)
                                               * q_scale).astype(BF16)
        k_ref[:, q0:q0 + C_NOPE] = k_nope[:, a * C_NOPE:(a + 1) * C_NOPE].astype(BF16)
        k_ref[:, q0 + C_NOPE:q0 + HEAD_PAD] = k_rope
    vt_ref[...] = lax.dot_general(wvt_ref[...], ckv, (((1,), (1,)), ((), ())),
                                  preferred_element_type=F32).astype(BF16)


def _rope_lane_tables(seq):
    half = C_ROPE // 2
    inv = 1.0 / (ROPE_BASE ** (jnp.arange(0, C_ROPE, 2, dtype=F32) / C_ROPE))
    ang = jnp.arange(seq, dtype=F32)[:, None] * inv[None, :]
    cos, sin = jnp.cos(ang), jnp.sin(ang)
    zero = jnp.zeros((seq, half), F32)
    pad = jnp.zeros((seq, LANES - C_ROPE), F32)
    cos_t = jnp.concatenate([cos, cos, pad], axis=1)
    sin_lo = jnp.concatenate([-sin, zero, pad], axis=1)
    sin_hi = jnp.concatenate([zero, sin, pad], axis=1)
    return cos_t, sin_lo, sin_hi


def _mla_proj(x, ln, w_dq, q_norm, w_uq, w_dkv, kv_norm, w_ukv, seq):
    t, d = x.shape
    q_lora = w_dq.shape[1]
    wuq = w_uq.reshape(q_lora, C_HEADS, C_QK)
    wuq = jnp.pad(wuq, ((0, 0), (0, 0), (0, HEAD_PAD - C_QK))).reshape(q_lora, C_HEADS * HEAD_PAD).astype(BF16)
    wdkv = jnp.pad(w_dkv, ((0, 0), (0, LANES - C_ROPE))).astype(BF16)
    wukv = w_ukv.reshape(C_KV_LORA, C_HEADS, C_NOPE + C_V)
    wuk = wukv[:, :, :C_NOPE].reshape(C_KV_LORA, -1).astype(BF16)
    wvt = wukv[:, :, C_NOPE:].reshape(C_KV_LORA, -1).T.astype(BF16)
    cos_t, sin_lo, sin_hi = _rope_lane_tables(seq)
    tps = seq // TM
    tab = pl.BlockSpec((TM, LANES), lambda i: (i % tps, 0))
    tok = lambda w: pl.BlockSpec((TM, w), lambda i: (i, 0))
    kern = functools.partial(_mla_proj_kernel, q_scale=(C_QK ** -0.5) * math.log2(math.e))
    return pl.pallas_call(
        kern,
        grid=(t // TM,),
        in_specs=[tok(d), _resident((1, d)), _resident(w_dq.shape), _resident((1, q_lora)), _resident(wuq.shape),
                  _resident(wdkv.shape), _resident((1, C_KV_LORA)), _resident(wuk.shape), _resident(wvt.shape),
                  tab, tab, tab],
        out_specs=[tok(C_HEADS * HEAD_PAD), tok(C_HEADS * HEAD_PAD),
                   pl.BlockSpec((None, C_HEADS * C_V, TM), lambda i: (i // tps, 0, i % tps))],
        out_shape=[jax.ShapeDtypeStruct((t, C_HEADS * HEAD_PAD), BF16),
                   jax.ShapeDtypeStruct((t, C_HEADS * HEAD_PAD), BF16),
                   jax.ShapeDtypeStruct((t // seq, C_HEADS * C_V, seq), BF16)],
        compiler_params=_params(),
        name="mla_proj",
    )(x, ln.reshape(1, d), w_dq.astype(BF16), q_norm.reshape(1, -1), wuq, wdkv, kv_norm.reshape(1, -1), wuk, wvt,
      cos_t, sin_lo, sin_hi)


def _col_reduce(x, op):
    rows = x.shape[0]
    while rows > SUBLANES:
        rows //= 2
        x = op(x[:rows], x[rows:])
    if op is jnp.add:
        return jnp.sum(x, axis=0, keepdims=True)
    return jnp.max(x, axis=0, keepdims=True)


def _attn_kernel(q_ref, k_ref, vt_ref, o_ref, s_ref, m_ref, l_ref, acc_ref, *, n_kv):
    heads = range(HEADS_PER_STEP)
    m_ref[...] = jnp.full(m_ref.shape, -jnp.inf, F32)
    l_ref[...] = jnp.zeros(l_ref.shape, F32)
    acc_ref[...] = jnp.zeros(acc_ref.shape, F32)

    def scores(j, a):
        r0 = pl.multiple_of(j * TK, TK)
        q = q_ref[:, a * HEAD_PAD:(a + 1) * HEAD_PAD]
        k = k_ref[pl.ds(r0, TK), a * HEAD_PAD:(a + 1) * HEAD_PAD]
        return lax.dot_general(k, q, (((1,), (1,)), ((), ())), preferred_element_type=F32)

    def consume(j, a, s):
        r0 = pl.multiple_of(j * TK, TK)
        m_old = m_ref[a]
        m_new = jnp.maximum(m_old, _col_reduce(s, jnp.maximum))
        p = jnp.exp2(s - m_new)
        alpha = jnp.exp2(m_old - m_new)
        l_ref[a] = alpha * l_ref[a] + _col_reduce(p, jnp.add)
        pv = _dot(vt_ref[a * C_V:(a + 1) * C_V, pl.ds(r0, TK)], p.astype(BF16))
        acc_ref[a] = alpha * acc_ref[a] + pv
        m_ref[a] = m_new

    for a in heads:
        s_ref[a] = scores(0, a)

    def step(j, carry):
        s_cur = [s_ref[a] for a in heads]
        s_next = [scores(j + 1, a) for a in heads]
        for a in heads:
            consume(j, a, s_cur[a])
        for a in heads:
            s_ref[a] = s_next[a]
        return carry

    lax.fori_loop(0, n_kv - 1, step, 0, unroll=3)
    for a in heads:
        consume(n_kv - 1, a, s_ref[a])
        o_ref[:, a * C_V:(a + 1) * C_V] = (acc_ref[a] / l_ref[a]).T.astype(o_ref.dtype)


def _attention(q, k, vt, batch, seq):
    t = q.shape[0]
    nq = seq // TQ
    hs = HEADS_PER_STEP
    kern = functools.partial(_attn_kernel, n_kv=seq // TK)
    return pl.pallas_call(
        kern,
        grid=(batch, C_HEADS // hs, nq),
        in_specs=[pl.BlockSpec((TQ, hs * HEAD_PAD), lambda b, a, i: (b * nq + i, a)),
                  pl.BlockSpec((seq, hs * HEAD_PAD), lambda b, a, i: (b, a)),
                  pl.BlockSpec((None, hs * C_V, seq), lambda b, a, i: (b, a, 0))],
        out_specs=pl.BlockSpec((TQ, hs * C_V), lambda b, a, i: (b * nq + i, a)),
        out_shape=jax.ShapeDtypeStruct((t, C_HEADS * C_V), BF16),
        scratch_shapes=[pltpu.VMEM((hs, TK, TQ), F32), pltpu.VMEM((hs, 1, TQ), F32), pltpu.VMEM((hs, 1, TQ), F32),
                        pltpu.VMEM((hs, C_V, TQ), F32)],
        compiler_params=_params(3),
        name="mla_attn",
    )(q, k, vt)


def _out_proj_kernel(x_ref, o_ref, w_ref, y_ref):
    y_ref[...] = x_ref[...] + _dot(o_ref[...], w_ref[...])


def _out_proj(x, o, w_o):
    t, d = x.shape
    return pl.pallas_call(
        _out_proj_kernel,
        grid=(t // TM,),
        in_specs=[pl.BlockSpec((TM, d), lambda i: (i, 0)), pl.BlockSpec((TM, o.shape[1]), lambda i: (i, 0)),
                  _resident(w_o.shape)],
        out_specs=pl.BlockSpec((TM, d), lambda i: (i, 0)),
        out_shape=jax.ShapeDtypeStruct((t, d), F32),
        compiler_params=_params(),
        name="mla_out",
    )(x, o, w_o.astype(BF16))


def _trunk(x3, p):
    batch, seq, d = x3.shape
    assert seq % TM == 0 and seq % TQ == 0 and seq % TK == 0 and TM % A_CHUNK == 0
    x = x3.reshape(batch * seq, d)
    depth = p["ln_mix"].shape[0]
    for i in range(depth):
        j, kind = divmod(i, 3)
        if kind == 0:
            x = _mixer_a(x, p["ln_mix"][i], p["a_w_in"][j], p["a_b_in"][j], p["a_v_norm"][j], p["a_w_s"][j],
                         p["a_b_s"][j], p["a_w_out"][j])
        elif kind == 1:
            x = _mixer_b(x, p["ln_mix"][i], p["b_w_grp"][j], p["b_scale"][j], seq)
        else:
            q, k, v = _mla_proj(x, p["ln_mix"][i], p["c_w_dq"][j], p["c_q_norm"][j], p["c_w_uq"][j],
                                p["c_w_dkv"][j], p["c_kv_norm"][j], p["c_w_ukv"][j], seq)
            x = _out_proj(x, _attention(q, k, v, batch, seq), p["c_w_o"][j])
        x = _ffn(x, p["ln_ffn"][i], p["f_w_up"][i], p["f_conv_w"][i], p["f_conv_b"][i], p["f_w_down"][i],
                 p["ln_final"], seq, final_norm=(i == depth - 1))
    return x.reshape(batch, seq, d)


def kernel(x_prompt, x_sample, ln_mix, ln_ffn, ln_final, a_w_in, a_b_in, a_v_norm, a_w_s, a_b_s, a_w_out, b_w_grp,
           b_scale, c_w_dq, c_q_norm, c_w_uq, c_w_dkv, c_kv_norm, c_w_ukv, c_w_o, f_w_up, f_conv_w, f_conv_b,
           f_w_down):
    p = dict(ln_mix=ln_mix, ln_ffn=ln_ffn, ln_final=ln_final, a_w_in=a_w_in, a_b_in=a_b_in, a_v_norm=a_v_norm,
             a_w_s=a_w_s, a_b_s=a_b_s, a_w_out=a_w_out, b_w_grp=b_w_grp, b_scale=b_scale, c_w_dq=c_w_dq,
             c_q_norm=c_q_norm, c_w_uq=c_w_uq, c_w_dkv=c_w_dkv, c_kv_norm=c_kv_norm, c_w_ukv=c_w_ukv, c_w_o=c_w_o,
             f_w_up=f_w_up, f_conv_w=f_conv_w, f_conv_b=f_conv_b, f_w_down=f_w_down)
    return (_trunk(x_prompt, p), _trunk(x_sample, p))
```

```python
import functools
import math

import jax
import jax.numpy as jnp
import numpy as np
from jax import lax
from jax.experimental import pallas as pl
from jax.experimental.pallas import tpu as pltpu

F32 = jnp.float32
BF16 = jnp.bfloat16

EPS = 1e-6
A_GROUPS = 8
A_CHUNK = 128
POOL_WINDOWS = (2, 4, 8, 16)
C_HEADS = 8
C_NOPE = 128
C_ROPE = 64
C_V = 128
C_KV_LORA = 256
C_QK = C_NOPE + C_ROPE
ROPE_BASE = 10000.0

LANES = 128
SUBLANES = 8
HALO = SUBLANES
HEAD_PAD = 2 * LANES
VMEM_LIMIT = 56 * 1024 * 1024

TM = 512
FF_CHUNK = 256
POOL_SUB = 128
TQ = 512
TK = 512
HEADS_PER_STEP = 2


def _dot(a, b):
    return jnp.dot(a, b, preferred_element_type=F32)


def _rms(x, g):
    return x * lax.rsqrt(jnp.mean(x * x, axis=-1, keepdims=True) + EPS) * g


def _params(n_axes=1):
    return pltpu.CompilerParams(dimension_semantics=("arbitrary",) * n_axes, vmem_limit_bytes=VMEM_LIMIT)


def _resident(shape):
    nd = len(shape)
    return pl.BlockSpec(shape, lambda *_: (0,) * nd, pipeline_mode=pl.Buffered(1))


def _halo_specs(tm, d, n_tiles):
    r = tm // HALO
    last = n_tiles * r - 1
    prev = pl.BlockSpec((HALO, d), lambda i: (jnp.maximum(i * r - 1, 0), 0))
    main = pl.BlockSpec((tm, d), lambda i: (i, 0))
    nxt = pl.BlockSpec((HALO, d), lambda i: (jnp.minimum((i + 1) * r, last), 0))
    return prev, main, nxt


def _ffn_kernel(xp_ref, x_ref, xn_ref, g_ref, wup_ref, cw_ref, cb_ref, wdn_ref, gf_ref, o_ref, hs_ref, ys_ref, act_ref,
                *, tiles_per_seq, n_chunks, final_norm):
    tm, d = x_ref.shape
    ck = FF_CHUNK
    rows = tm + 2 * HALO
    n_rt = rows // SUBLANES
    n_slab = d // LANES
    j = pl.program_id(0) % tiles_per_seq
    keep_prev = (j != 0).astype(F32)
    keep_next = (j != tiles_per_seq - 1).astype(F32)
    g = g_ref[...]
    for lo, hi, h in ((0, HALO, _rms(xp_ref[...], g) * keep_prev), (HALO, HALO + tm, _rms(x_ref[...], g)),
                      (HALO + tm, rows, _rms(xn_ref[...], g) * keep_next)):
        for k in range(n_slab):
            hs_ref[k, lo:hi, :] = h[:, k * LANES:(k + 1) * LANES]
    hperm = jnp.concatenate(
        [jnp.concatenate([hs_ref[k, pl.ds(i, SUBLANES, stride=n_rt), :] for k in range(n_slab)], axis=1)
         for i in range(n_rt)], axis=0).astype(BF16)
    for c in range(n_chunks):
        z = _dot(hperm, wup_ref[c])
        cw = cw_ref[c]
        zp = jnp.concatenate([pltpu.roll(z[rows - SUBLANES:], 1, 0), z[:rows - SUBLANES]], axis=0)
        zn = jnp.concatenate([z[SUBLANES:], pltpu.roll(z[:SUBLANES], SUBLANES - 1, 0)], axis=0)
        zz = zp * cw[0:1] + z * cw[1:2] + zn * cw[2:3] + cb_ref[c]
        gate, val = zz[:, :ck], zz[:, ck:]
        act_ref[:, c * ck:(c + 1) * ck] = (gate * jax.nn.sigmoid(gate) * val).astype(BF16)
    yp = _dot(act_ref[...], wdn_ref[...])
    for i in range(n_rt):
        for k in range(n_slab):
            ys_ref[k, pl.ds(i, SUBLANES, stride=n_rt), :] = yp[i * SUBLANES:(i + 1) * SUBLANES,
                                                               k * LANES:(k + 1) * LANES]
    y = x_ref[...] + jnp.concatenate([ys_ref[k, HALO:HALO + tm, :] for k in range(n_slab)], axis=1)
    if final_norm:
        y = _rms(y, gf_ref[...])
    o_ref[...] = y


def _ffn(x, ln, w_up, conv_w, conv_b, w_down, ln_final, seq, final_norm):
    t, d = x.shape
    f = w_down.shape[0]
    n_chunks = f // FF_CHUNK
    n_tiles = t // TM

    def pair(a):
        r = a.shape[0]
        return a.reshape(r, 2, n_chunks, FF_CHUNK).transpose(2, 0, 1, 3).reshape(n_chunks, r, 2 * FF_CHUNK)

    wup = pair(w_up.astype(BF16))
    cw = pair(conv_w)
    cb = pair(conv_b.reshape(1, -1))
    prev, main, nxt = _halo_specs(TM, d, n_tiles)
    kern = functools.partial(_ffn_kernel, tiles_per_seq=seq // TM, n_chunks=n_chunks, final_norm=final_norm)
    return pl.pallas_call(
        kern,
        grid=(n_tiles,),
        in_specs=[prev, main, nxt, _resident((1, d)), _resident(wup.shape), _resident(cw.shape),
                  _resident(cb.shape), _resident(w_down.shape), _resident((1, d))],
        out_specs=pl.BlockSpec((TM, d), lambda i: (i, 0)),
        out_shape=jax.ShapeDtypeStruct((t, d), F32),
        scratch_shapes=[pltpu.VMEM((d // LANES, TM + 2 * HALO, LANES), F32),
                        pltpu.VMEM((d // LANES, TM + 2 * HALO, LANES), F32),
                        pltpu.VMEM((TM + 2 * HALO, f), BF16)],
        compiler_params=_params(),
        name="ffn",
    )(x, x, x, ln.reshape(1, d), wup, cw, cb, w_down.astype(BF16), ln_final.reshape(1, d))


def _mixer_a_kernel(x_ref, g_ref, win_ref, bin_ref, vn_ref, ws_ref, bs_ref, wout_ref, o_ref, sv_ref):
    tm = x_ref.shape[0]
    width = wout_ref.shape[0]
    gw = width // A_GROUPS
    x = x_ref[...]
    h = _rms(x, g_ref[...]).astype(BF16)
    z = _dot(h, win_ref[...]) + bin_ref[...]
    z = 0.5 * z * (1.0 + lax.erf(z * (1.0 / math.sqrt(2.0))))
    u = z[:, :width]
    v = _rms(z[:, width:], vn_ref[...]).astype(BF16)
    for n in range(tm // A_CHUNK):
        r0 = n * A_CHUNK
        for a in range(A_GROUPS):
            c0 = a * gw
            sv_ref[r0:r0 + A_CHUNK, c0:c0 + gw] = _dot(ws_ref[a], v[r0:r0 + A_CHUNK, c0:c0 + gw]) + bs_ref[a]
    o_ref[...] = x + _dot((u * sv_ref[...]).astype(BF16), wout_ref[...])


def _mixer_a(x, ln, w_in, b_in, v_norm, w_s, b_s, w_out):
    t, d = x.shape
    width = w_out.shape[0]
    gw = width // A_GROUPS
    bsb = jnp.broadcast_to(b_s[:, :, None], (A_GROUPS, A_CHUNK, gw))
    return pl.pallas_call(
        _mixer_a_kernel,
        grid=(t // TM,),
        in_specs=[pl.BlockSpec((TM, d), lambda i: (i, 0)), _resident((1, d)), _resident(w_in.shape),
                  _resident((1, 2 * width)), _resident((1, width)), _resident(w_s.shape), _resident(bsb.shape),
                  _resident(w_out.shape)],
        out_specs=pl.BlockSpec((TM, d), lambda i: (i, 0)),
        out_shape=jax.ShapeDtypeStruct((t, d), F32),
        scratch_shapes=[pltpu.VMEM((TM, width), F32)],
        compiler_params=_params(),
        name="mixer_a",
    )(x, ln.reshape(1, d), w_in.astype(BF16), b_in.reshape(1, -1), v_norm.reshape(1, -1), w_s.astype(BF16), bsb,
      w_out.astype(BF16))


def _mixer_b_kernel(xp_ref, x_ref, xn_ref, g_ref, pool_ref, wg_ref, sc_ref, o_ref, *, tiles_per_seq, seq):
    tm, d = x_ref.shape
    gw = d // len(POOL_WINDOWS)
    j = pl.program_id(0) % tiles_per_seq
    keep_prev = (j != 0).astype(F32)
    keep_next = (j != tiles_per_seq - 1).astype(F32)
    g = g_ref[...]
    x = x_ref[...]
    h = _rms(x, g)
    hext = jnp.concatenate([_rms(xp_ref[...], g) * keep_prev, h, _rms(xn_ref[...], g) * keep_next], axis=0)
    hi = hext.astype(BF16)
    lo = (hext - hi.astype(F32)).astype(BF16)
    pos = j * tm + lax.broadcasted_iota(jnp.int32, (tm, 1), 0)
    sub, span = pool_ref.shape[1], pool_ref.shape[2]
    for a, w in enumerate(POOL_WINDOWS):
        c0 = a * gw
        band = pool_ref[a]
        wsum = jnp.concatenate([_dot(band, hi[r0:r0 + span, c0:c0 + gw]) + _dot(band, lo[r0:r0 + span, c0:c0 + gw])
                                for r0 in range(0, tm, sub)], axis=0)
        cnt = (jnp.minimum(pos + w // 2, seq) - jnp.maximum(pos - w // 2, 0)).astype(F32)
        pooled = (wsum / cnt - h[:, c0:c0 + gw]).astype(BF16)
        o_ref[:, c0:c0 + gw] = x[:, c0:c0 + gw] + _dot(pooled, wg_ref[a]) * sc_ref[:, c0:c0 + gw]


def _pool_bands(tm):
    t = np.arange(tm)[:, None] + HALO
    jj = np.arange(tm + 2 * HALO)[None, :]
    return np.stack([((jj >= t - w // 2) & (jj < t + w // 2)) for w in POOL_WINDOWS]).astype(np.float32)


def _mixer_b(x, ln, w_grp, scale, seq):
    t, d = x.shape
    n_tiles = t // TM
    bands = jnp.asarray(_pool_bands(POOL_SUB), BF16)
    prev, main, nxt = _halo_specs(TM, d, n_tiles)
    kern = functools.partial(_mixer_b_kernel, tiles_per_seq=seq // TM, seq=seq)
    return pl.pallas_call(
        kern,
        grid=(n_tiles,),
        in_specs=[prev, main, nxt, _resident((1, d)), _resident(bands.shape), _resident(w_grp.shape),
                  _resident((1, d))],
        out_specs=pl.BlockSpec((TM, d), lambda i: (i, 0)),
        out_shape=jax.ShapeDtypeStruct((t, d), F32),
        compiler_params=_params(),
        name="mixer_b",
    )(x, x, x, ln.reshape(1, d), bands, w_grp.astype(BF16), scale.reshape(1, d))


def _rope(r, cos, sin_lo, sin_hi):
    half = C_ROPE // 2
    return r * cos + pltpu.roll(r, LANES - half, 1) * sin_lo + pltpu.roll(r, half, 1) * sin_hi


def _mla_proj_kernel(x_ref, g_ref, wdq_ref, qn_ref, wuqt_ref, wdkv_ref, kvn_ref, wuk_ref, wvt_ref, cos_ref, slo_ref,
                     shi_ref, cost_ref, sint_ref, qt_ref, k_ref, vt_ref, *, q_scale):
    nt = (((1,), (1,)), ((), ()))
    half = C_ROPE // 2
    h = _rms(x_ref[...], g_ref[...]).astype(BF16)
    cq = _rms(_dot(h, wdq_ref[...]), qn_ref[...]).astype(BF16)
    qt = lax.dot_general(wuqt_ref[...], cq, nt, preferred_element_type=F32)
    kv = _dot(h, wdkv_ref[...])
    ckv = _rms(kv[:, :C_KV_LORA], kvn_ref[...]).astype(BF16)
    k_rope = _rope(kv[:, C_KV_LORA:], cos_ref[...], slo_ref[...], shi_ref[...]).astype(BF16)
    k_nope = _dot(ckv, wuk_ref[...])
    cos_t, sin_t = cost_ref[...], sint_ref[...]
    for a in range(C_HEADS):
        q0 = a * HEAD_PAD
        r0 = q0 + C_NOPE
        x1, x2 = qt[r0:r0 + half], qt[r0 + half:r0 + C_ROPE]
        qt_ref[q0:r0, :] = (qt[q0:r0] * q_scale).astype(BF16)
        qt_ref[r0:r0 + half, :] = ((x1 * cos_t - x2 * sin_t) * q_scale).astype(BF16)
        qt_ref[r0 + half:r0 + C_ROPE, :] = ((x1 * sin_t + x2 * cos_t) * q_scale).astype(BF16)
        qt_ref[r0 + C_ROPE:q0 + HEAD_PAD, :] = qt[r0 + C_ROPE:q0 + HEAD_PAD].astype(BF16)
        k_ref[:, q0:q0 + C_NOPE] = k_nope[:, a * C_NOPE:(a + 1) * C_NOPE].astype(BF16)
        k_ref[:, q0 + C_NOPE:q0 + HEAD_PAD] = k_rope
    vt_ref[...] = lax.dot_general(wvt_ref[...], ckv, nt, preferred_element_type=F32).astype(BF16)


def _rope_lane_tables(seq):
    half = C_ROPE // 2
    inv = 1.0 / (ROPE_BASE ** (jnp.arange(0, C_ROPE, 2, dtype=F32) / C_ROPE))
    ang = jnp.arange(seq, dtype=F32)[:, None] * inv[None, :]
    cos, sin = jnp.cos(ang), jnp.sin(ang)
    zero = jnp.zeros((seq, half), F32)
    pad = jnp.zeros((seq, LANES - C_ROPE), F32)
    cos_t = jnp.concatenate([cos, cos, pad], axis=1)
    sin_lo = jnp.concatenate([-sin, zero, pad], axis=1)
    sin_hi = jnp.concatenate([zero, sin, pad], axis=1)
    return cos_t, sin_lo, sin_hi, cos.T, sin.T


def _mla_proj(x, ln, w_dq, q_norm, w_uq, w_dkv, kv_norm, w_ukv, seq):
    t, d = x.shape
    q_lora = w_dq.shape[1]
    wuq = w_uq.reshape(q_lora, C_HEADS, C_QK)
    wuqt = jnp.pad(wuq, ((0, 0), (0, 0), (0, HEAD_PAD - C_QK))).reshape(q_lora, C_HEADS * HEAD_PAD).T.astype(BF16)
    wdkv = jnp.pad(w_dkv, ((0, 0), (0, LANES - C_ROPE))).astype(BF16)
    wukv = w_ukv.reshape(C_KV_LORA, C_HEADS, C_NOPE + C_V)
    wuk = wukv[:, :, :C_NOPE].reshape(C_KV_LORA, -1).astype(BF16)
    wvt = wukv[:, :, C_NOPE:].reshape(C_KV_LORA, -1).T.astype(BF16)
    cos_l, sin_lo, sin_hi, cos_t, sin_t = _rope_lane_tables(seq)
    tps = seq // TM
    tab = pl.BlockSpec((TM, LANES), lambda i: (i % tps, 0))
    tab_t = pl.BlockSpec((C_ROPE // 2, TM), lambda i: (0, i % tps))
    feat = lambda n: pl.BlockSpec((None, n, TM), lambda i: (i // tps, 0, i % tps))
    tok = lambda w: pl.BlockSpec((TM, w), lambda i: (i, 0))
    kern = functools.partial(_mla_proj_kernel, q_scale=(C_QK ** -0.5) * math.log2(math.e))
    return pl.pallas_call(
        kern,
        grid=(t // TM,),
        in_specs=[tok(d), _resident((1, d)), _resident(w_dq.shape), _resident((1, q_lora)), _resident(wuqt.shape),
                  _resident(wdkv.shape), _resident((1, C_KV_LORA)), _resident(wuk.shape), _resident(wvt.shape),
                  tab, tab, tab, tab_t, tab_t],
        out_specs=[feat(C_HEADS * HEAD_PAD), tok(C_HEADS * HEAD_PAD), feat(C_HEADS * C_V)],
        out_shape=[jax.ShapeDtypeStruct((t // seq, C_HEADS * HEAD_PAD, seq), BF16),
                   jax.ShapeDtypeStruct((t, C_HEADS * HEAD_PAD), BF16),
                   jax.ShapeDtypeStruct((t // seq, C_HEADS * C_V, seq), BF16)],
        compiler_params=_params(),
        name="mla_proj",
    )(x, ln.reshape(1, d), w_dq.astype(BF16), q_norm.reshape(1, -1), wuqt, wdkv, kv_norm.reshape(1, -1), wuk, wvt,
      cos_l, sin_lo, sin_hi, cos_t, sin_t)


def _col_reduce(x, op):
    rows = x.shape[0]
    while rows > SUBLANES:
        rows //= 2
        x = op(x[:rows], x[rows:])
    if op is jnp.add:
        return jnp.sum(x, axis=0, keepdims=True)
    return jnp.max(x, axis=0, keepdims=True)


def _attn_kernel(qt_ref, k_ref, vt_ref, o_ref, s_ref, m_ref, l_ref, acc_ref, *, n_kv):
    heads = range(HEADS_PER_STEP)
    m_ref[...] = jnp.full(m_ref.shape, -jnp.inf, F32)
    l_ref[...] = jnp.zeros(l_ref.shape, F32)
    acc_ref[...] = jnp.zeros(acc_ref.shape, F32)

    def scores(j, a):
        r0 = pl.multiple_of(j * TK, TK)
        k = k_ref[pl.ds(r0, TK), a * HEAD_PAD:(a + 1) * HEAD_PAD]
        return _dot(k, qt_ref[a * HEAD_PAD:(a + 1) * HEAD_PAD, :])

    def consume(j, a, s):
        r0 = pl.multiple_of(j * TK, TK)
        m_old = m_ref[a]
        m_new = jnp.maximum(m_old, _col_reduce(s, jnp.maximum))
        p = jnp.exp2(s - m_new)
        alpha = jnp.exp2(m_old - m_new)
        l_ref[a] = alpha * l_ref[a] + _col_reduce(p, jnp.add)
        pv = _dot(vt_ref[a * C_V:(a + 1) * C_V, pl.ds(r0, TK)], p.astype(BF16))
        acc_ref[a] = alpha * acc_ref[a] + pv
        m_ref[a] = m_new

    for a in heads:
        s_ref[a] = scores(0, a)

    def step(j, carry):
        s_cur = [s_ref[a] for a in heads]
        s_next = [scores(j + 1, a) for a in heads]
        for a in heads:
            consume(j, a, s_cur[a])
        for a in heads:
            s_ref[a] = s_next[a]
        return carry

    lax.fori_loop(0, n_kv - 1, step, 0, unroll=3)
    for a in heads:
        consume(n_kv - 1, a, s_ref[a])
        o_ref[:, a * C_V:(a + 1) * C_V] = (acc_ref[a] / l_ref[a]).T.astype(o_ref.dtype)


def _attention(qt, k, vt, batch, seq):
    t = k.shape[0]
    nq = seq // TQ
    hs = HEADS_PER_STEP
    kern = functools.partial(_attn_kernel, n_kv=seq // TK)
    return pl.pallas_call(
        kern,
        grid=(batch, C_HEADS // hs, nq),
        in_specs=[pl.BlockSpec((None, hs * HEAD_PAD, TQ), lambda b, a, i: (b, a, i)),
                  pl.BlockSpec((seq, hs * HEAD_PAD), lambda b, a, i: (b, a)),
                  pl.BlockSpec((None, hs * C_V, seq), lambda b, a, i: (b, a, 0))],
        out_specs=pl.BlockSpec((TQ, hs * C_V), lambda b, a, i: (b * nq + i, a)),
        out_shape=jax.ShapeDtypeStruct((t, C_HEADS * C_V), BF16),
        scratch_shapes=[pltpu.VMEM((hs, TK, TQ), F32), pltpu.VMEM((hs, 1, TQ), F32), pltpu.VMEM((hs, 1, TQ), F32),
                        pltpu.VMEM((hs, C_V, TQ), F32)],
        compiler_params=_params(3),
        name="mla_attn",
    )(qt, k, vt)


def _out_proj_kernel(x_ref, o_ref, w_ref, y_ref):
    y_ref[...] = x_ref[...] + _dot(o_ref[...], w_ref[...])


def _out_proj(x, o, w_o):
    t, d = x.shape
    return pl.pallas_call(
        _out_proj_kernel,
        grid=(t // TM,),
        in_specs=[pl.BlockSpec((TM, d), lambda i: (i, 0)), pl.BlockSpec((TM, o.shape[1]), lambda i: (i, 0)),
                  _resident(w_o.shape)],
        out_specs=pl.BlockSpec((TM, d), lambda i: (i, 0)),
        out_shape=jax.ShapeDtypeStruct((t, d), F32),
        compiler_params=_params(),
        name="mla_out",
    )(x, o, w_o.astype(BF16))


def _trunk(x3, p):
    batch, seq, d = x3.shape
    assert seq % TM == 0 and seq % TQ == 0 and seq % TK == 0 and TM % A_CHUNK == 0
    x = x3.reshape(batch * seq, d)
    depth = p["ln_mix"].shape[0]
    for i in range(depth):
        j, kind = divmod(i, 3)
        if kind == 0:
            x = _mixer_a(x, p["ln_mix"][i], p["a_w_in"][j], p["a_b_in"][j], p["a_v_norm"][j], p["a_w_s"][j],
                         p["a_b_s"][j], p["a_w_out"][j])
        elif kind == 1:
            x = _mixer_b(x, p["ln_mix"][i], p["b_w_grp"][j], p["b_scale"][j], seq)
        else:
            q, k, v = _mla_proj(x, p["ln_mix"][i], p["c_w_dq"][j], p["c_q_norm"][j], p["c_w_uq"][j],
                                p["c_w_dkv"][j], p["c_kv_norm"][j], p["c_w_ukv"][j], seq)
            x = _out_proj(x, _attention(q, k, v, batch, seq), p["c_w_o"][j])
        x = _ffn(x, p["ln_ffn"][i], p["f_w_up"][i], p["f_conv_w"][i], p["f_conv_b"][i], p["f_w_down"][i],
                 p["ln_final"], seq, final_norm=(i == depth - 1))
    return x.reshape(batch, seq, d)


def kernel(x_prompt, x_sample, ln_mix, ln_ffn, ln_final, a_w_in, a_b_in, a_v_norm, a_w_s, a_b_s, a_w_out, b_w_grp,
           b_scale, c_w_dq, c_q_norm, c_w_uq, c_w_dkv, c_kv_norm, c_w_ukv, c_w_o, f_w_up, f_conv_w, f_conv_b,
           f_w_down):
    p = dict(ln_mix=ln_mix, ln_ffn=ln_ffn, ln_final=ln_final, a_w_in=a_w_in, a_b_in=a_b_in, a_v_norm=a_v_norm,
             a_w_s=a_w_s, a_b_s=a_b_s, a_w_out=a_w_out, b_w_grp=b_w_grp, b_scale=b_scale, c_w_dq=c_w_dq,
             c_q_norm=c_q_norm, c_w_uq=c_w_uq, c_w_dkv=c_w_dkv, c_kv_norm=c_kv_norm, c_w_ukv=c_w_ukv, c_w_o=c_w_o,
             f_w_up=f_w_up, f_conv_w=f_conv_w, f_conv_b=f_conv_b, f_w_down=f_w_down)
    return (_trunk(x_prompt, p), _trunk(x_sample, p))
```

```python
import functools
import math

import jax
import jax.numpy as jnp
import numpy as np
from jax import lax
from jax.experimental import pallas as pl
from jax.experimental.pallas import tpu as pltpu

F32 = jnp.float32
BF16 = jnp.bfloat16

EPS = 1e-6
A_GROUPS = 8
A_CHUNK = 128
POOL_WINDOWS = (2, 4, 8, 16)
C_HEADS = 8
C_NOPE = 128
C_ROPE = 64
C_V = 128
C_KV_LORA = 256
C_QK = C_NOPE + C_ROPE
ROPE_BASE = 10000.0

LANES = 128
SUBLANES = 8
HALO = SUBLANES
HEAD_PAD = 2 * LANES
VMEM_LIMIT = 56 * 1024 * 1024

TM = 512
FF_CHUNK = 256
POOL_SUB = 128
TQ = 512
TK = 512
HEADS_PER_STEP = 2
ATTN_UNROLL = 3


def _dot(a, b):
    return jnp.dot(a, b, preferred_element_type=F32)


def _rms(x, g):
    return x * lax.rsqrt(jnp.mean(x * x, axis=-1, keepdims=True) + EPS) * g


def _params(n_axes=1):
    return pltpu.CompilerParams(dimension_semantics=("arbitrary",) * n_axes, vmem_limit_bytes=VMEM_LIMIT)


def _resident(shape):
    nd = len(shape)
    return pl.BlockSpec(shape, lambda *_: (0,) * nd, pipeline_mode=pl.Buffered(1))


def _halo_specs(tm, d, n_tiles):
    r = tm // HALO
    last = n_tiles * r - 1
    prev = pl.BlockSpec((HALO, d), lambda i: (jnp.maximum(i * r - 1, 0), 0))
    main = pl.BlockSpec((tm, d), lambda i: (i, 0))
    nxt = pl.BlockSpec((HALO, d), lambda i: (jnp.minimum((i + 1) * r, last), 0))
    return prev, main, nxt


def _ffn_kernel(xp_ref, x_ref, xn_ref, g_ref, wup_ref, cw_ref, cb_ref, wdn_ref, gf_ref, o_ref, hs_ref, ys_ref, act_ref,
                *, tiles_per_seq, n_chunks, final_norm):
    tm, d = x_ref.shape
    ck = FF_CHUNK
    rows = tm + 2 * HALO
    n_rt = rows // SUBLANES
    n_slab = d // LANES
    j = pl.program_id(0) % tiles_per_seq
    keep_prev = (j != 0).astype(F32)
    keep_next = (j != tiles_per_seq - 1).astype(F32)
    g = g_ref[...]
    for lo, hi, h in ((0, HALO, _rms(xp_ref[...], g) * keep_prev), (HALO, HALO + tm, _rms(x_ref[...], g)),
                      (HALO + tm, rows, _rms(xn_ref[...], g) * keep_next)):
        for k in range(n_slab):
            hs_ref[k, lo:hi, :] = h[:, k * LANES:(k + 1) * LANES]
    hperm = jnp.concatenate(
        [jnp.concatenate([hs_ref[k, pl.ds(i, SUBLANES, stride=n_rt), :] for k in range(n_slab)], axis=1)
         for i in range(n_rt)], axis=0).astype(BF16)
    for c in range(n_chunks):
        z = _dot(hperm, wup_ref[c])
        cw = cw_ref[c]
        zp = jnp.concatenate([pltpu.roll(z[rows - SUBLANES:], 1, 0), z[:rows - SUBLANES]], axis=0)
        zn = jnp.concatenate([z[SUBLANES:], pltpu.roll(z[:SUBLANES], SUBLANES - 1, 0)], axis=0)
        zz = zp * cw[0:1] + z * cw[1:2] + zn * cw[2:3] + cb_ref[c]
        gate, val = zz[:, :ck], zz[:, ck:]
        act_ref[:, c * ck:(c + 1) * ck] = (gate * jax.nn.sigmoid(gate) * val).astype(BF16)
    yp = _dot(act_ref[...], wdn_ref[...])
    for i in range(n_rt):
        for k in range(n_slab):
            ys_ref[k, pl.ds(i, SUBLANES, stride=n_rt), :] = yp[i * SUBLANES:(i + 1) * SUBLANES,
                                                               k * LANES:(k + 1) * LANES]
    y = x_ref[...] + jnp.concatenate([ys_ref[k, HALO:HALO + tm, :] for k in range(n_slab)], axis=1)
    if final_norm:
        y = _rms(y, gf_ref[...])
    o_ref[...] = y


def _ffn(x, ln, w_up, conv_w, conv_b, w_down, ln_final, seq, final_norm):
    t, d = x.shape
    f = w_down.shape[0]
    n_chunks = f // FF_CHUNK
    n_tiles = t // TM

    def pair(a):
        r = a.shape[0]
        return a.reshape(r, 2, n_chunks, FF_CHUNK).transpose(2, 0, 1, 3).reshape(n_chunks, r, 2 * FF_CHUNK)

    wup = pair(w_up.astype(BF16))
    cw = pair(conv_w)
    cb = pair(conv_b.reshape(1, -1))
    prev, main, nxt = _halo_specs(TM, d, n_tiles)
    kern = functools.partial(_ffn_kernel, tiles_per_seq=seq // TM, n_chunks=n_chunks, final_norm=final_norm)
    return pl.pallas_call(
        kern,
        grid=(n_tiles,),
        in_specs=[prev, main, nxt, _resident((1, d)), _resident(wup.shape), _resident(cw.shape),
                  _resident(cb.shape), _resident(w_down.shape), _resident((1, d))],
        out_specs=pl.BlockSpec((TM, d), lambda i: (i, 0)),
        out_shape=jax.ShapeDtypeStruct((t, d), F32),
        scratch_shapes=[pltpu.VMEM((d // LANES, TM + 2 * HALO, LANES), F32),
                        pltpu.VMEM((d // LANES, TM + 2 * HALO, LANES), F32),
                        pltpu.VMEM((TM + 2 * HALO, f), BF16)],
        compiler_params=_params(),
        name="ffn",
    )(x, x, x, ln.reshape(1, d), wup, cw, cb, w_down.astype(BF16), ln_final.reshape(1, d))


def _mixer_a_kernel(x_ref, g_ref, win_ref, bin_ref, vn_ref, ws_ref, bs_ref, wout_ref, o_ref, sv_ref):
    tm = x_ref.shape[0]
    width = wout_ref.shape[0]
    gw = width // A_GROUPS
    x = x_ref[...]
    h = _rms(x, g_ref[...]).astype(BF16)
    z = _dot(h, win_ref[...]) + bin_ref[...]
    z = 0.5 * z * (1.0 + lax.erf(z * (1.0 / math.sqrt(2.0))))
    u = z[:, :width]
    v = _rms(z[:, width:], vn_ref[...]).astype(BF16)
    for n in range(tm // A_CHUNK):
        r0 = n * A_CHUNK
        for a in range(A_GROUPS):
            c0 = a * gw
            sv_ref[r0:r0 + A_CHUNK, c0:c0 + gw] = _dot(ws_ref[a], v[r0:r0 + A_CHUNK, c0:c0 + gw]) + bs_ref[a]
    o_ref[...] = x + _dot((u * sv_ref[...]).astype(BF16), wout_ref[...])


def _mixer_a(x, ln, w_in, b_in, v_norm, w_s, b_s, w_out):
    t, d = x.shape
    width = w_out.shape[0]
    gw = width // A_GROUPS
    bsb = jnp.broadcast_to(b_s[:, :, None], (A_GROUPS, A_CHUNK, gw))
    return pl.pallas_call(
        _mixer_a_kernel,
        grid=(t // TM,),
        in_specs=[pl.BlockSpec((TM, d), lambda i: (i, 0)), _resident((1, d)), _resident(w_in.shape),
                  _resident((1, 2 * width)), _resident((1, width)), _resident(w_s.shape), _resident(bsb.shape),
                  _resident(w_out.shape)],
        out_specs=pl.BlockSpec((TM, d), lambda i: (i, 0)),
        out_shape=jax.ShapeDtypeStruct((t, d), F32),
        scratch_shapes=[pltpu.VMEM((TM, width), F32)],
        compiler_params=_params(),
        name="mixer_a",
    )(x, ln.reshape(1, d), w_in.astype(BF16), b_in.reshape(1, -1), v_norm.reshape(1, -1), w_s.astype(BF16), bsb,
      w_out.astype(BF16))


def _mixer_b_kernel(xp_ref, x_ref, xn_ref, g_ref, pool_ref, wg_ref, sc_ref, o_ref, *, tiles_per_seq, seq):
    tm, d = x_ref.shape
    gw = d // len(POOL_WINDOWS)
    j = pl.program_id(0) % tiles_per_seq
    keep_prev = (j != 0).astype(F32)
    keep_next = (j != tiles_per_seq - 1).astype(F32)
    g = g_ref[...]
    x = x_ref[...]
    h = _rms(x, g)
    hext = jnp.concatenate([_rms(xp_ref[...], g) * keep_prev, h, _rms(xn_ref[...], g) * keep_next], axis=0)
    hi = hext.astype(BF16)
    lo = (hext - hi.astype(F32)).astype(BF16)
    pos = j * tm + lax.broadcasted_iota(jnp.int32, (tm, 1), 0)
    sub, span = pool_ref.shape[1], pool_ref.shape[2]
    for a, w in enumerate(POOL_WINDOWS):
        c0 = a * gw
        band = pool_ref[a]
        wsum = jnp.concatenate([_dot(band, hi[r0:r0 + span, c0:c0 + gw]) + _dot(band, lo[r0:r0 + span, c0:c0 + gw])
                                for r0 in range(0, tm, sub)], axis=0)
        cnt = (jnp.minimum(pos + w // 2, seq) - jnp.maximum(pos - w // 2, 0)).astype(F32)
        pooled = (wsum / cnt - h[:, c0:c0 + gw]).astype(BF16)
        o_ref[:, c0:c0 + gw] = x[:, c0:c0 + gw] + _dot(pooled, wg_ref[a]) * sc_ref[:, c0:c0 + gw]


def _pool_bands(tm):
    t = np.arange(tm)[:, None] + HALO
    jj = np.arange(tm + 2 * HALO)[None, :]
    return np.stack([((jj >= t - w // 2) & (jj < t + w // 2)) for w in POOL_WINDOWS]).astype(np.float32)


def _mixer_b(x, ln, w_grp, scale, seq):
    t, d = x.shape
    n_tiles = t // TM
    bands = jnp.asarray(_pool_bands(POOL_SUB), BF16)
    prev, main, nxt = _halo_specs(TM, d, n_tiles)
    kern = functools.partial(_mixer_b_kernel, tiles_per_seq=seq // TM, seq=seq)
    return pl.pallas_call(
        kern,
        grid=(n_tiles,),
        in_specs=[prev, main, nxt, _resident((1, d)), _resident(bands.shape), _resident(w_grp.shape),
                  _resident((1, d))],
        out_specs=pl.BlockSpec((TM, d), lambda i: (i, 0)),
        out_shape=jax.ShapeDtypeStruct((t, d), F32),
        compiler_params=_params(),
        name="mixer_b",
    )(x, x, x, ln.reshape(1, d), bands, w_grp.astype(BF16), scale.reshape(1, d))


def _rope(r, cos, sin_lo, sin_hi):
    half = C_ROPE // 2
    return r * cos + pltpu.roll(r, LANES - half, 1) * sin_lo + pltpu.roll(r, half, 1) * sin_hi


def _mla_proj_kernel(x_ref, g_ref, wdq_ref, qn_ref, wuqt_ref, wdkv_ref, kvn_ref, wuk_ref, wvt_ref, cos_ref, slo_ref,
                     shi_ref, cost_ref, sint_ref, qt_ref, k_ref, vt_ref, *, q_scale):
    nt = (((1,), (1,)), ((), ()))
    half = C_ROPE // 2
    h = _rms(x_ref[...], g_ref[...]).astype(BF16)
    cq = _rms(_dot(h, wdq_ref[...]), qn_ref[...]).astype(BF16)
    qt = lax.dot_general(wuqt_ref[...], cq, nt, preferred_element_type=F32)
    kv = _dot(h, wdkv_ref[...])
    ckv = _rms(kv[:, :C_KV_LORA], kvn_ref[...]).astype(BF16)
    k_rope = _rope(kv[:, C_KV_LORA:], cos_ref[...], slo_ref[...], shi_ref[...]).astype(BF16)
    k_nope = _dot(ckv, wuk_ref[...])
    cos_t, sin_t = cost_ref[...], sint_ref[...]
    for a in range(C_HEADS):
        q0 = a * HEAD_PAD
        r0 = q0 + C_NOPE
        x1, x2 = qt[r0:r0 + half], qt[r0 + half:r0 + C_ROPE]
        qt_ref[q0:r0, :] = (qt[q0:r0] * q_scale).astype(BF16)
        qt_ref[r0:r0 + half, :] = ((x1 * cos_t - x2 * sin_t) * q_scale).astype(BF16)
        qt_ref[r0 + half:r0 + C_ROPE, :] = ((x1 * sin_t + x2 * cos_t) * q_scale).astype(BF16)
        qt_ref[r0 + C_ROPE:q0 + HEAD_PAD, :] = qt[r0 + C_ROPE:q0 + HEAD_PAD].astype(BF16)
        k_ref[:, q0:q0 + C_NOPE] = k_nope[:, a * C_NOPE:(a + 1) * C_NOPE].astype(BF16)
        k_ref[:, q0 + C_NOPE:q0 + HEAD_PAD] = k_rope
    vt_ref[...] = lax.dot_general(wvt_ref[...], ckv, nt, preferred_element_type=F32).astype(BF16)


def _rope_lane_tables(seq):
    half = C_ROPE // 2
    inv = 1.0 / (ROPE_BASE ** (jnp.arange(0, C_ROPE, 2, dtype=F32) / C_ROPE))
    ang = jnp.arange(seq, dtype=F32)[:, None] * inv[None, :]
    cos, sin = jnp.cos(ang), jnp.sin(ang)
    zero = jnp.zeros((seq, half), F32)
    pad = jnp.zeros((seq, LANES - C_ROPE), F32)
    cos_t = jnp.concatenate([cos, cos, pad], axis=1)
    sin_lo = jnp.concatenate([-sin, zero, pad], axis=1)
    sin_hi = jnp.concatenate([zero, sin, pad], axis=1)
    return cos_t, sin_lo, sin_hi, cos.T, sin.T


def _mla_proj(x, ln, w_dq, q_norm, w_uq, w_dkv, kv_norm, w_ukv, seq):
    t, d = x.shape
    q_lora = w_dq.shape[1]
    wuq = w_uq.reshape(q_lora, C_HEADS, C_QK)
    wuqt = jnp.pad(wuq, ((0, 0), (0, 0), (0, HEAD_PAD - C_QK))).reshape(q_lora, C_HEADS * HEAD_PAD).T.astype(BF16)
    wdkv = jnp.pad(w_dkv, ((0, 0), (0, LANES - C_ROPE))).astype(BF16)
    wukv = w_ukv.reshape(C_KV_LORA, C_HEADS, C_NOPE + C_V)
    wuk = wukv[:, :, :C_NOPE].reshape(C_KV_LORA, -1).astype(BF16)
    wvt = wukv[:, :, C_NOPE:].reshape(C_KV_LORA, -1).T.astype(BF16)
    cos_l, sin_lo, sin_hi, cos_t, sin_t = _rope_lane_tables(seq)
    tps = seq // TM
    tab = pl.BlockSpec((TM, LANES), lambda i: (i % tps, 0))
    tab_t = pl.BlockSpec((C_ROPE // 2, TM), lambda i: (0, i % tps))
    feat = lambda n: pl.BlockSpec((None, n, TM), lambda i: (i // tps, 0, i % tps))
    tok = lambda w: pl.BlockSpec((TM, w), lambda i: (i, 0))
    kern = functools.partial(_mla_proj_kernel, q_scale=(C_QK ** -0.5) * math.log2(math.e))
    return pl.pallas_call(
        kern,
        grid=(t // TM,),
        in_specs=[tok(d), _resident((1, d)), _resident(w_dq.shape), _resident((1, q_lora)), _resident(wuqt.shape),
                  _resident(wdkv.shape), _resident((1, C_KV_LORA)), _resident(wuk.shape), _resident(wvt.shape),
                  tab, tab, tab, tab_t, tab_t],
        out_specs=[feat(C_HEADS * HEAD_PAD), tok(C_HEADS * HEAD_PAD), feat(C_HEADS * C_V)],
        out_shape=[jax.ShapeDtypeStruct((t // seq, C_HEADS * HEAD_PAD, seq), BF16),
                   jax.ShapeDtypeStruct((t, C_HEADS * HEAD_PAD), BF16),
                   jax.ShapeDtypeStruct((t // seq, C_HEADS * C_V, seq), BF16)],
        compiler_params=_params(),
        name="mla_proj",
    )(x, ln.reshape(1, d), w_dq.astype(BF16), q_norm.reshape(1, -1), wuqt, wdkv, kv_norm.reshape(1, -1), wuk, wvt,
      cos_l, sin_lo, sin_hi, cos_t, sin_t)


def _col_reduce(x, op, n_acc=4):
    tiles = [x[r:r + SUBLANES] for r in range(0, x.shape[0], SUBLANES)]
    accs = tiles[:n_acc]
    for i, tile in enumerate(tiles[n_acc:]):
        accs[i % n_acc] = op(accs[i % n_acc], tile)
    while len(accs) > 1:
        accs = [op(accs[i], accs[i + 1]) for i in range(0, len(accs), 2)]
    if op is jnp.add:
        return jnp.sum(accs[0], axis=0, keepdims=True)
    return jnp.max(accs[0], axis=0, keepdims=True)


def _attn_kernel(qt_ref, k_ref, vt_ref, o_ref, s_ref, smax_ref, m_ref, l_ref, acc_ref, *, n_kv):
    heads = range(HEADS_PER_STEP)
    m_ref[...] = jnp.full(m_ref.shape, -jnp.inf, F32)
    l_ref[...] = jnp.zeros(l_ref.shape, F32)
    acc_ref[...] = jnp.zeros(acc_ref.shape, F32)

    def scores(j, a):
        r0 = pl.multiple_of(j * TK, TK)
        k = k_ref[pl.ds(r0, TK), a * HEAD_PAD:(a + 1) * HEAD_PAD]
        return _dot(k, qt_ref[a * HEAD_PAD:(a + 1) * HEAD_PAD, :])

    def consume(j, a, s, s_max):
        r0 = pl.multiple_of(j * TK, TK)
        m_old = m_ref[a]
        m_new = jnp.maximum(m_old, s_max)
        p = jnp.exp2(s - m_new)
        alpha = jnp.exp2(m_old - m_new)
        l_ref[a] = alpha * l_ref[a] + _col_reduce(p, jnp.add)
        pv = _dot(vt_ref[a * C_V:(a + 1) * C_V, pl.ds(r0, TK)], p.astype(BF16))
        acc_ref[a] = alpha * acc_ref[a] + pv
        m_ref[a] = m_new

    def produce(j, a):
        s = scores(j, a)
        s_ref[a] = s
        smax_ref[a] = _col_reduce(s, jnp.maximum)

    for a in heads:
        produce(0, a)

    def step(j, carry):
        cur = [(s_ref[a], smax_ref[a]) for a in heads]
        nxt = [scores(j + 1, a) for a in heads]
        nxt_max = [_col_reduce(s, jnp.maximum) for s in nxt]
        for a in heads:
            consume(j, a, *cur[a])
        for a in heads:
            s_ref[a] = nxt[a]
            smax_ref[a] = nxt_max[a]
        return carry

    lax.fori_loop(0, n_kv - 1, step, 0, unroll=ATTN_UNROLL if (n_kv - 1) > ATTN_UNROLL else 1)
    for a in heads:
        consume(n_kv - 1, a, s_ref[a], smax_ref[a])
        o_ref[:, a * C_V:(a + 1) * C_V] = (acc_ref[a] / l_ref[a]).T.astype(o_ref.dtype)


def _attention(qt, k, vt, batch, seq):
    t = k.shape[0]
    nq = seq // TQ
    hs = HEADS_PER_STEP
    kern = functools.partial(_attn_kernel, n_kv=seq // TK)
    return pl.pallas_call(
        kern,
        grid=(batch, C_HEADS // hs, nq),
        in_specs=[pl.BlockSpec((None, hs * HEAD_PAD, TQ), lambda b, a, i: (b, a, i)),
                  pl.BlockSpec((seq, hs * HEAD_PAD), lambda b, a, i: (b, a)),
                  pl.BlockSpec((None, hs * C_V, seq), lambda b, a, i: (b, a, 0))],
        out_specs=pl.BlockSpec((TQ, hs * C_V), lambda b, a, i: (b * nq + i, a)),
        out_shape=jax.ShapeDtypeStruct((t, C_HEADS * C_V), BF16),
        scratch_shapes=[pltpu.VMEM((hs, TK, TQ), F32), pltpu.VMEM((hs, 1, TQ), F32), pltpu.VMEM((hs, 1, TQ), F32),
                        pltpu.VMEM((hs, 1, TQ), F32), pltpu.VMEM((hs, C_V, TQ), F32)],
        compiler_params=_params(3),
        name="mla_attn",
    )(qt, k, vt)


def _out_proj_kernel(x_ref, o_ref, w_ref, y_ref):
    y_ref[...] = x_ref[...] + _dot(o_ref[...], w_ref[...])


def _out_proj(x, o, w_o):
    t, d = x.shape
    return pl.pallas_call(
        _out_proj_kernel,
        grid=(t // TM,),
        in_specs=[pl.BlockSpec((TM, d), lambda i: (i, 0)), pl.BlockSpec((TM, o.shape[1]), lambda i: (i, 0)),
                  _resident(w_o.shape)],
        out_specs=pl.BlockSpec((TM, d), lambda i: (i, 0)),
        out_shape=jax.ShapeDtypeStruct((t, d), F32),
        compiler_params=_params(),
        name="mla_out",
    )(x, o, w_o.astype(BF16))


def _trunk(x3, p):
    batch, seq, d = x3.shape
    assert seq % TM == 0 and seq % TQ == 0 and seq % TK == 0 and TM % A_CHUNK == 0
    x = x3.reshape(batch * seq, d)
    depth = p["ln_mix"].shape[0]
    for i in range(depth):
        j, kind = divmod(i, 3)
        if kind == 0:
            x = _mixer_a(x, p["ln_mix"][i], p["a_w_in"][j], p["a_b_in"][j], p["a_v_norm"][j], p["a_w_s"][j],
                         p["a_b_s"][j], p["a_w_out"][j])
        elif kind == 1:
            x = _mixer_b(x, p["ln_mix"][i], p["b_w_grp"][j], p["b_scale"][j], seq)
        else:
            q, k, v = _mla_proj(x, p["ln_mix"][i], p["c_w_dq"][j], p["c_q_norm"][j], p["c_w_uq"][j],
                                p["c_w_dkv"][j], p["c_kv_norm"][j], p["c_w_ukv"][j], seq)
            x = _out_proj(x, _attention(q, k, v, batch, seq), p["c_w_o"][j])
        x = _ffn(x, p["ln_ffn"][i], p["f_w_up"][i], p["f_conv_w"][i], p["f_conv_b"][i], p["f_w_down"][i],
                 p["ln_final"], seq, final_norm=(i == depth - 1))
    return x.reshape(batch, seq, d)


def kernel(x_prompt, x_sample, ln_mix, ln_ffn, ln_final, a_w_in, a_b_in, a_v_norm, a_w_s, a_b_s, a_w_out, b_w_grp,
           b_scale, c_w_dq, c_q_norm, c_w_uq, c_w_dkv, c_kv_norm, c_w_ukv, c_w_o, f_w_up, f_conv_w, f_conv_b,
           f_w_down):
    p = dict(ln_mix=ln_mix, ln_ffn=ln_ffn, ln_final=ln_final, a_w_in=a_w_in, a_b_in=a_b_in, a_v_norm=a_v_norm,
             a_w_s=a_w_s, a_b_s=a_b_s, a_w_out=a_w_out, b_w_grp=b_w_grp, b_scale=b_scale, c_w_dq=c_w_dq,
             c_q_norm=c_q_norm, c_w_uq=c_w_uq, c_w_dkv=c_w_dkv, c_kv_norm=c_kv_norm, c_w_ukv=c_w_ukv, c_w_o=c_w_o,
             f_w_up=f_w_up, f_conv_w=f_conv_w, f_conv_b=f_conv_b, f_w_down=f_w_down)
    return (_trunk(x_prompt, p), _trunk(x_sample, p))
```

```python
import functools
import math

import jax
import jax.numpy as jnp
import numpy as np
from jax import lax
from jax.experimental import pallas as pl
from jax.experimental.pallas import tpu as pltpu

F32 = jnp.float32
BF16 = jnp.bfloat16

EPS = 1e-6
A_GROUPS = 8
A_CHUNK = 128
POOL_WINDOWS = (2, 4, 8, 16)
C_HEADS = 8
C_NOPE = 128
C_ROPE = 64
C_V = 128
C_KV_LORA = 256
C_QK = C_NOPE + C_ROPE
ROPE_BASE = 10000.0

LANES = 128
SUBLANES = 8
HALO = SUBLANES
HEAD_PAD = 2 * LANES
VMEM_LIMIT = 56 * 1024 * 1024

TM = 1024
FF_CHUNK = 256
POOL_SUB = 128
TQ = 512
TK = 512
HEADS_PER_STEP = 2
ATTN_UNROLL = 3


def _dot(a, b):
    return jnp.dot(a, b, preferred_element_type=F32)


def _rms(x, g):
    return x * lax.rsqrt(jnp.mean(x * x, axis=-1, keepdims=True) + EPS) * g


def _params(n_axes=1):
    return pltpu.CompilerParams(dimension_semantics=("arbitrary",) * n_axes, vmem_limit_bytes=VMEM_LIMIT)


def _resident(shape):
    nd = len(shape)
    return pl.BlockSpec(shape, lambda *_: (0,) * nd, pipeline_mode=pl.Buffered(1))


def _halo_specs(tm, d, n_tiles):
    r = tm // HALO
    last = n_tiles * r - 1
    prev = pl.BlockSpec((HALO, d), lambda i: (jnp.maximum(i * r - 1, 0), 0))
    main = pl.BlockSpec((tm, d), lambda i: (i, 0))
    nxt = pl.BlockSpec((HALO, d), lambda i: (jnp.minimum((i + 1) * r, last), 0))
    return prev, main, nxt


def _ffn_kernel(xp_ref, x_ref, xn_ref, g_ref, wup_ref, cw_ref, cb_ref, wdn_ref, gf_ref, o_ref, hs_ref, ys_ref, act_ref,
                *, tiles_per_seq, n_chunks, final_norm):
    tm, d = x_ref.shape
    ck = FF_CHUNK
    rows = tm + 2 * HALO
    n_rt = rows // SUBLANES
    n_slab = d // LANES
    j = pl.program_id(0) % tiles_per_seq
    keep_prev = (j != 0).astype(F32)
    keep_next = (j != tiles_per_seq - 1).astype(F32)
    g = g_ref[...]
    for lo, hi, h in ((0, HALO, _rms(xp_ref[...], g) * keep_prev), (HALO, HALO + tm, _rms(x_ref[...], g)),
                      (HALO + tm, rows, _rms(xn_ref[...], g) * keep_next)):
        for k in range(n_slab):
            hs_ref[k, lo:hi, :] = h[:, k * LANES:(k + 1) * LANES]
    hperm = jnp.concatenate(
        [jnp.concatenate([hs_ref[k, pl.ds(i, SUBLANES, stride=n_rt), :] for k in range(n_slab)], axis=1)
         for i in range(n_rt)], axis=0).astype(BF16)
    for c in range(n_chunks):
        z = _dot(hperm, wup_ref[c])
        cw = cw_ref[c]
        zp = jnp.concatenate([pltpu.roll(z[rows - SUBLANES:], 1, 0), z[:rows - SUBLANES]], axis=0)
        zn = jnp.concatenate([z[SUBLANES:], pltpu.roll(z[:SUBLANES], SUBLANES - 1, 0)], axis=0)
        zz = zp * cw[0:1] + z * cw[1:2] + zn * cw[2:3] + cb_ref[c]
        gate, val = zz[:, :ck], zz[:, ck:]
        act_ref[:, c * ck:(c + 1) * ck] = (gate * jax.nn.sigmoid(gate) * val).astype(BF16)
    yp = _dot(act_ref[...], wdn_ref[...])
    for i in range(n_rt):
        for k in range(n_slab):
            ys_ref[k, pl.ds(i, SUBLANES, stride=n_rt), :] = yp[i * SUBLANES:(i + 1) * SUBLANES,
                                                               k * LANES:(k + 1) * LANES]
    y = x_ref[...] + jnp.concatenate([ys_ref[k, HALO:HALO + tm, :] for k in range(n_slab)], axis=1)
    if final_norm:
        y = _rms(y, gf_ref[...])
    o_ref[...] = y


def _ffn(x, ln, w_up, conv_w, conv_b, w_down, ln_final, seq, final_norm):
    t, d = x.shape
    f = w_down.shape[0]
    n_chunks = f // FF_CHUNK
    n_tiles = t // TM

    def pair(a):
        r = a.shape[0]
        return a.reshape(r, 2, n_chunks, FF_CHUNK).transpose(2, 0, 1, 3).reshape(n_chunks, r, 2 * FF_CHUNK)

    wup = pair(w_up.astype(BF16))
    cw = pair(conv_w)
    cb = pair(conv_b.reshape(1, -1))
    prev, main, nxt = _halo_specs(TM, d, n_tiles)
    kern = functools.partial(_ffn_kernel, tiles_per_seq=seq // TM, n_chunks=n_chunks, final_norm=final_norm)
    return pl.pallas_call(
        kern,
        grid=(n_tiles,),
        in_specs=[prev, main, nxt, _resident((1, d)), _resident(wup.shape), _resident(cw.shape),
                  _resident(cb.shape), _resident(w_down.shape), _resident((1, d))],
        out_specs=pl.BlockSpec((TM, d), lambda i: (i, 0)),
        out_shape=jax.ShapeDtypeStruct((t, d), F32),
        scratch_shapes=[pltpu.VMEM((d // LANES, TM + 2 * HALO, LANES), F32),
                        pltpu.VMEM((d // LANES, TM + 2 * HALO, LANES), F32),
                        pltpu.VMEM((TM + 2 * HALO, f), BF16)],
        compiler_params=_params(),
        name="ffn",
    )(x, x, x, ln.reshape(1, d), wup, cw, cb, w_down.astype(BF16), ln_final.reshape(1, d))


def _mixer_a_kernel(x_ref, g_ref, win_ref, bin_ref, vn_ref, ws_ref, bs_ref, wout_ref, o_ref, sv_ref):
    tm = x_ref.shape[0]
    width = wout_ref.shape[0]
    gw = width // A_GROUPS
    x = x_ref[...]
    h = _rms(x, g_ref[...]).astype(BF16)
    z = _dot(h, win_ref[...]) + bin_ref[...]
    z = 0.5 * z * (1.0 + lax.erf(z * (1.0 / math.sqrt(2.0))))
    u = z[:, :width]
    v = _rms(z[:, width:], vn_ref[...]).astype(BF16)
    for n in range(tm // A_CHUNK):
        r0 = n * A_CHUNK
        for a in range(A_GROUPS):
            c0 = a * gw
            sv_ref[r0:r0 + A_CHUNK, c0:c0 + gw] = _dot(ws_ref[a], v[r0:r0 + A_CHUNK, c0:c0 + gw]) + bs_ref[a]
    o_ref[...] = x + _dot((u * sv_ref[...]).astype(BF16), wout_ref[...])


def _mixer_a(x, ln, w_in, b_in, v_norm, w_s, b_s, w_out):
    t, d = x.shape
    width = w_out.shape[0]
    gw = width // A_GROUPS
    bsb = jnp.broadcast_to(b_s[:, :, None], (A_GROUPS, A_CHUNK, gw))
    return pl.pallas_call(
        _mixer_a_kernel,
        grid=(t // TM,),
        in_specs=[pl.BlockSpec((TM, d), lambda i: (i, 0)), _resident((1, d)), _resident(w_in.shape),
                  _resident((1, 2 * width)), _resident((1, width)), _resident(w_s.shape), _resident(bsb.shape),
                  _resident(w_out.shape)],
        out_specs=pl.BlockSpec((TM, d), lambda i: (i, 0)),
        out_shape=jax.ShapeDtypeStruct((t, d), F32),
        scratch_shapes=[pltpu.VMEM((TM, width), F32)],
        compiler_params=_params(),
        name="mixer_a",
    )(x, ln.reshape(1, d), w_in.astype(BF16), b_in.reshape(1, -1), v_norm.reshape(1, -1), w_s.astype(BF16), bsb,
      w_out.astype(BF16))


def _mixer_b_kernel(xp_ref, x_ref, xn_ref, g_ref, pool_ref, wg_ref, sc_ref, o_ref, *, tiles_per_seq, seq):
    tm, d = x_ref.shape
    gw = d // len(POOL_WINDOWS)
    j = pl.program_id(0) % tiles_per_seq
    keep_prev = (j != 0).astype(F32)
    keep_next = (j != tiles_per_seq - 1).astype(F32)
    g = g_ref[...]
    x = x_ref[...]
    h = _rms(x, g)
    hext = jnp.concatenate([_rms(xp_ref[...], g) * keep_prev, h, _rms(xn_ref[...], g) * keep_next], axis=0)
    hi = hext.astype(BF16)
    lo = (hext - hi.astype(F32)).astype(BF16)
    pos = j * tm + lax.broadcasted_iota(jnp.int32, (tm, 1), 0)
    sub, span = pool_ref.shape[1], pool_ref.shape[2]
    for a, w in enumerate(POOL_WINDOWS):
        c0 = a * gw
        band = pool_ref[a]
        wsum = jnp.concatenate([_dot(band, hi[r0:r0 + span, c0:c0 + gw]) + _dot(band, lo[r0:r0 + span, c0:c0 + gw])
                                for r0 in range(0, tm, sub)], axis=0)
        cnt = (jnp.minimum(pos + w // 2, seq) - jnp.maximum(pos - w // 2, 0)).astype(F32)
        pooled = (wsum / cnt - h[:, c0:c0 + gw]).astype(BF16)
        o_ref[:, c0:c0 + gw] = x[:, c0:c0 + gw] + _dot(pooled, wg_ref[a]) * sc_ref[:, c0:c0 + gw]


def _pool_bands(tm):
    t = np.arange(tm)[:, None] + HALO
    jj = np.arange(tm + 2 * HALO)[None, :]
    return np.stack([((jj >= t - w // 2) & (jj < t + w // 2)) for w in POOL_WINDOWS]).astype(np.float32)


def _mixer_b(x, ln, w_grp, scale, seq):
    t, d = x.shape
    n_tiles = t // TM
    bands = jnp.asarray(_pool_bands(POOL_SUB), BF16)
    prev, main, nxt = _halo_specs(TM, d, n_tiles)
    kern = functools.partial(_mixer_b_kernel, tiles_per_seq=seq // TM, seq=seq)
    return pl.pallas_call(
        kern,
        grid=(n_tiles,),
        in_specs=[prev, main, nxt, _resident((1, d)), _resident(bands.shape), _resident(w_grp.shape),
                  _resident((1, d))],
        out_specs=pl.BlockSpec((TM, d), lambda i: (i, 0)),
        out_shape=jax.ShapeDtypeStruct((t, d), F32),
        compiler_params=_params(),
        name="mixer_b",
    )(x, x, x, ln.reshape(1, d), bands, w_grp.astype(BF16), scale.reshape(1, d))


def _rope(r, cos, sin_lo, sin_hi):
    half = C_ROPE // 2
    return r * cos + pltpu.roll(r, LANES - half, 1) * sin_lo + pltpu.roll(r, half, 1) * sin_hi


def _mla_proj_kernel(x_ref, g_ref, wdq_ref, qn_ref, wuqt_ref, wdkv_ref, kvn_ref, wuk_ref, wvt_ref, cos_ref, slo_ref,
                     shi_ref, cost_ref, sint_ref, qt_ref, k_ref, vt_ref, *, q_scale):
    nt = (((1,), (1,)), ((), ()))
    half = C_ROPE // 2
    h = _rms(x_ref[...], g_ref[...]).astype(BF16)
    cq = _rms(_dot(h, wdq_ref[...]), qn_ref[...]).astype(BF16)
    qt = lax.dot_general(wuqt_ref[...], cq, nt, preferred_element_type=F32)
    kv = _dot(h, wdkv_ref[...])
    ckv = _rms(kv[:, :C_KV_LORA], kvn_ref[...]).astype(BF16)
    k_rope = _rope(kv[:, C_KV_LORA:], cos_ref[...], slo_ref[...], shi_ref[...]).astype(BF16)
    k_nope = _dot(ckv, wuk_ref[...])
    cos_t, sin_t = cost_ref[...], sint_ref[...]
    for a in range(C_HEADS):
        q0 = a * HEAD_PAD
        r0 = q0 + C_NOPE
        x1, x2 = qt[r0:r0 + half], qt[r0 + half:r0 + C_ROPE]
        qt_ref[q0:r0, :] = (qt[q0:r0] * q_scale).astype(BF16)
        qt_ref[r0:r0 + half, :] = ((x1 * cos_t - x2 * sin_t) * q_scale).astype(BF16)
        qt_ref[r0 + half:r0 + C_ROPE, :] = ((x1 * sin_t + x2 * cos_t) * q_scale).astype(BF16)
        qt_ref[r0 + C_ROPE:q0 + HEAD_PAD, :] = qt[r0 + C_ROPE:q0 + HEAD_PAD].astype(BF16)
        k_ref[:, q0:q0 + C_NOPE] = k_nope[:, a * C_NOPE:(a + 1) * C_NOPE].astype(BF16)
        k_ref[:, q0 + C_NOPE:q0 + HEAD_PAD] = k_rope
    vt_ref[...] = lax.dot_general(wvt_ref[...], ckv, nt, preferred_element_type=F32).astype(BF16)


def _rope_lane_tables(seq):
    half = C_ROPE // 2
    inv = 1.0 / (ROPE_BASE ** (jnp.arange(0, C_ROPE, 2, dtype=F32) / C_ROPE))
    ang = jnp.arange(seq, dtype=F32)[:, None] * inv[None, :]
    cos, sin = jnp.cos(ang), jnp.sin(ang)
    zero = jnp.zeros((seq, half), F32)
    pad = jnp.zeros((seq, LANES - C_ROPE), F32)
    cos_t = jnp.concatenate([cos, cos, pad], axis=1)
    sin_lo = jnp.concatenate([-sin, zero, pad], axis=1)
    sin_hi = jnp.concatenate([zero, sin, pad], axis=1)
    return cos_t, sin_lo, sin_hi, cos.T, sin.T


def _mla_proj(x, ln, w_dq, q_norm, w_uq, w_dkv, kv_norm, w_ukv, seq):
    t, d = x.shape
    q_lora = w_dq.shape[1]
    wuq = w_uq.reshape(q_lora, C_HEADS, C_QK)
    wuqt = jnp.pad(wuq, ((0, 0), (0, 0), (0, HEAD_PAD - C_QK))).reshape(q_lora, C_HEADS * HEAD_PAD).T.astype(BF16)
    wdkv = jnp.pad(w_dkv, ((0, 0), (0, LANES - C_ROPE))).astype(BF16)
    wukv = w_ukv.reshape(C_KV_LORA, C_HEADS, C_NOPE + C_V)
    wuk = wukv[:, :, :C_NOPE].reshape(C_KV_LORA, -1).astype(BF16)
    wvt = wukv[:, :, C_NOPE:].reshape(C_KV_LORA, -1).T.astype(BF16)
    cos_l, sin_lo, sin_hi, cos_t, sin_t = _rope_lane_tables(seq)
    tps = seq // TM
    tab = pl.BlockSpec((TM, LANES), lambda i: (i % tps, 0))
    tab_t = pl.BlockSpec((C_ROPE // 2, TM), lambda i: (0, i % tps))
    feat = lambda n: pl.BlockSpec((None, n, TM), lambda i: (i // tps, 0, i % tps))
    tok = lambda w: pl.BlockSpec((TM, w), lambda i: (i, 0))
    kern = functools.partial(_mla_proj_kernel, q_scale=(C_QK ** -0.5) * math.log2(math.e))
    return pl.pallas_call(
        kern,
        grid=(t // TM,),
        in_specs=[tok(d), _resident((1, d)), _resident(w_dq.shape), _resident((1, q_lora)), _resident(wuqt.shape),
                  _resident(wdkv.shape), _resident((1, C_KV_LORA)), _resident(wuk.shape), _resident(wvt.shape),
                  tab, tab, tab, tab_t, tab_t],
        out_specs=[feat(C_HEADS * HEAD_PAD), tok(C_HEADS * HEAD_PAD), feat(C_HEADS * C_V)],
        out_shape=[jax.ShapeDtypeStruct((t // seq, C_HEADS * HEAD_PAD, seq), BF16),
                   jax.ShapeDtypeStruct((t, C_HEADS * HEAD_PAD), BF16),
                   jax.ShapeDtypeStruct((t // seq, C_HEADS * C_V, seq), BF16)],
        compiler_params=_params(),
        name="mla_proj",
    )(x, ln.reshape(1, d), w_dq.astype(BF16), q_norm.reshape(1, -1), wuqt, wdkv, kv_norm.reshape(1, -1), wuk, wvt,
      cos_l, sin_lo, sin_hi, cos_t, sin_t)


def _col_reduce(x, op, n_acc=2):
    tiles = [x[r:r + SUBLANES] for r in range(0, x.shape[0], SUBLANES)]
    accs = tiles[:n_acc]
    for i, tile in enumerate(tiles[n_acc:]):
        accs[i % n_acc] = op(accs[i % n_acc], tile)
    while len(accs) > 1:
        accs = [op(accs[i], accs[i + 1]) for i in range(0, len(accs), 2)]
    if op is jnp.add:
        return jnp.sum(accs[0], axis=0, keepdims=True)
    return jnp.max(accs[0], axis=0, keepdims=True)


def _attn_kernel(qt_ref, k_ref, vt_ref, o_ref, s_ref, smax_ref, m_ref, l_ref, acc_ref, *, n_kv):
    heads = range(HEADS_PER_STEP)
    m_ref[...] = jnp.full(m_ref.shape, -jnp.inf, F32)
    l_ref[...] = jnp.zeros(l_ref.shape, F32)
    acc_ref[...] = jnp.zeros(acc_ref.shape, F32)

    def scores(j, a):
        r0 = pl.multiple_of(j * TK, TK)
        k = k_ref[pl.ds(r0, TK), a * HEAD_PAD:(a + 1) * HEAD_PAD]
        return _dot(k, qt_ref[a * HEAD_PAD:(a + 1) * HEAD_PAD, :])

    def consume(j, a, s, s_max):
        r0 = pl.multiple_of(j * TK, TK)
        m_old = m_ref[a]
        m_new = jnp.maximum(m_old, s_max)
        p = jnp.exp2(s - m_new)
        alpha = jnp.exp2(m_old - m_new)
        l_ref[a] = alpha * l_ref[a] + _col_reduce(p, jnp.add)
        pv = _dot(vt_ref[a * C_V:(a + 1) * C_V, pl.ds(r0, TK)], p.astype(BF16))
        acc_ref[a] = alpha * acc_ref[a] + pv
        m_ref[a] = m_new

    def produce(j, a):
        s = scores(j, a)
        s_ref[a] = s
        smax_ref[a] = _col_reduce(s, jnp.maximum)

    for a in heads:
        produce(0, a)

    def step(j, carry):
        cur = [(s_ref[a], smax_ref[a]) for a in heads]
        nxt = [scores(j + 1, a) for a in heads]
        nxt_max = [_col_reduce(s, jnp.maximum) for s in nxt]
        for a in heads:
            consume(j, a, *cur[a])
        for a in heads:
            s_ref[a] = nxt[a]
            smax_ref[a] = nxt_max[a]
        return carry

    lax.fori_loop(0, n_kv - 1, step, 0, unroll=ATTN_UNROLL if (n_kv - 1) > ATTN_UNROLL else 1)
    for a in heads:
        consume(n_kv - 1, a, s_ref[a], smax_ref[a])
        o_ref[:, a * C_V:(a + 1) * C_V] = (acc_ref[a] / l_ref[a]).T.astype(o_ref.dtype)


def _attention(qt, k, vt, batch, seq):
    t = k.shape[0]
    nq = seq // TQ
    hs = HEADS_PER_STEP
    kern = functools.partial(_attn_kernel, n_kv=seq // TK)
    return pl.pallas_call(
        kern,
        grid=(batch, C_HEADS // hs, nq),
        in_specs=[pl.BlockSpec((None, hs * HEAD_PAD, TQ), lambda b, a, i: (b, a, i)),
                  pl.BlockSpec((seq, hs * HEAD_PAD), lambda b, a, i: (b, a)),
                  pl.BlockSpec((None, hs * C_V, seq), lambda b, a, i: (b, a, 0))],
        out_specs=pl.BlockSpec((TQ, hs * C_V), lambda b, a, i: (b * nq + i, a)),
        out_shape=jax.ShapeDtypeStruct((t, C_HEADS * C_V), BF16),
        scratch_shapes=[pltpu.VMEM((hs, TK, TQ), F32), pltpu.VMEM((hs, 1, TQ), F32), pltpu.VMEM((hs, 1, TQ), F32),
                        pltpu.VMEM((hs, 1, TQ), F32), pltpu.VMEM((hs, C_V, TQ), F32)],
        compiler_params=_params(3),
        name="mla_attn",
    )(qt, k, vt)


def _out_proj_kernel(x_ref, o_ref, w_ref, y_ref):
    y_ref[...] = x_ref[...] + _dot(o_ref[...], w_ref[...])


def _out_proj(x, o, w_o):
    t, d = x.shape
    return pl.pallas_call(
        _out_proj_kernel,
        grid=(t // TM,),
        in_specs=[pl.BlockSpec((TM, d), lambda i: (i, 0)), pl.BlockSpec((TM, o.shape[1]), lambda i: (i, 0)),
                  _resident(w_o.shape)],
        out_specs=pl.BlockSpec((TM, d), lambda i: (i, 0)),
        out_shape=jax.ShapeDtypeStruct((t, d), F32),
        compiler_params=_params(),
        name="mla_out",
    )(x, o, w_o.astype(BF16))


def _trunk(x3, p):
    batch, seq, d = x3.shape
    assert seq % TM == 0 and seq % TQ == 0 and seq % TK == 0 and TM % A_CHUNK == 0
    x = x3.reshape(batch * seq, d)
    depth = p["ln_mix"].shape[0]
    for i in range(depth):
        j, kind = divmod(i, 3)
        if kind == 0:
            x = _mixer_a(x, p["ln_mix"][i], p["a_w_in"][j], p["a_b_in"][j], p["a_v_norm"][j], p["a_w_s"][j],
                         p["a_b_s"][j], p["a_w_out"][j])
        elif kind == 1:
            x = _mixer_b(x, p["ln_mix"][i], p["b_w_grp"][j], p["b_scale"][j], seq)
        else:
            q, k, v = _mla_proj(x, p["ln_mix"][i], p["c_w_dq"][j], p["c_q_norm"][j], p["c_w_uq"][j],
                                p["c_w_dkv"][j], p["c_kv_norm"][j], p["c_w_ukv"][j], seq)
            x = _out_proj(x, _attention(q, k, v, batch, seq), p["c_w_o"][j])
        x = _ffn(x, p["ln_ffn"][i], p["f_w_up"][i], p["f_conv_w"][i], p["f_conv_b"][i], p["f_w_down"][i],
                 p["ln_final"], seq, final_norm=(i == depth - 1))
    return x.reshape(batch, seq, d)


def kernel(x_prompt, x_sample, ln_mix, ln_ffn, ln_final, a_w_in, a_b_in, a_v_norm, a_w_s, a_b_s, a_w_out, b_w_grp,
           b_scale, c_w_dq, c_q_norm, c_w_uq, c_w_dkv, c_kv_norm, c_w_ukv, c_w_o, f_w_up, f_conv_w, f_conv_b,
           f_w_down):
    p = dict(ln_mix=ln_mix, ln_ffn=ln_ffn, ln_final=ln_final, a_w_in=a_w_in, a_b_in=a_b_in, a_v_norm=a_v_norm,
             a_w_s=a_w_s, a_b_s=a_b_s, a_w_out=a_w_out, b_w_grp=b_w_grp, b_scale=b_scale, c_w_dq=c_w_dq,
             c_q_norm=c_q_norm, c_w_uq=c_w_uq, c_w_dkv=c_w_dkv, c_kv_norm=c_kv_norm, c_w_ukv=c_w_ukv, c_w_o=c_w_o,
             f_w_up=f_w_up, f_conv_w=f_conv_w, f_conv_b=f_conv_b, f_w_down=f_w_down)
    return (_trunk(x_prompt, p), _trunk(x_sample, p))
```

```python
import functools
import math

import jax
import jax.numpy as jnp
import numpy as np
from jax import lax
from jax.experimental import pallas as pl
from jax.experimental.pallas import tpu as pltpu

F32 = jnp.float32
BF16 = jnp.bfloat16

EPS = 1e-6
A_GROUPS = 8
A_CHUNK = 128
POOL_WINDOWS = (2, 4, 8, 16)
C_HEADS = 8
C_NOPE = 128
C_ROPE = 64
C_V = 128
C_KV_LORA = 256
C_QK = C_NOPE + C_ROPE
ROPE_BASE = 10000.0

LANES = 128
SUBLANES = 8
HALO = SUBLANES
HEAD_PAD = 2 * LANES
VMEM_LIMIT = 56 * 1024 * 1024

TM = 1024
FF_CHUNK = 256
POOL_SUB = 128
TQ = 512
TK = 512
HEADS_PER_STEP = 2
Q_PER_STEP = 4
ATTN_UNROLL = 3


def _dot(a, b):
    return jnp.dot(a, b, preferred_element_type=F32)


def _rms(x, g):
    return x * lax.rsqrt(jnp.mean(x * x, axis=-1, keepdims=True) + EPS) * g


def _params(n_axes=1):
    return pltpu.CompilerParams(dimension_semantics=("arbitrary",) * n_axes, vmem_limit_bytes=VMEM_LIMIT)


def _resident(shape):
    nd = len(shape)
    return pl.BlockSpec(shape, lambda *_: (0,) * nd, pipeline_mode=pl.Buffered(1))


def _halo_specs(tm, d, n_tiles):
    r = tm // HALO
    last = n_tiles * r - 1
    prev = pl.BlockSpec((HALO, d), lambda i: (jnp.maximum(i * r - 1, 0), 0))
    main = pl.BlockSpec((tm, d), lambda i: (i, 0))
    nxt = pl.BlockSpec((HALO, d), lambda i: (jnp.minimum((i + 1) * r, last), 0))
    return prev, main, nxt


def _ffn_kernel(xp_ref, x_ref, xn_ref, g_ref, wup_ref, cw_ref, cb_ref, wdn_ref, gf_ref, o_ref, hs_ref, ys_ref, act_ref,
                *, tiles_per_seq, n_chunks, final_norm):
    tm, d = x_ref.shape
    ck = FF_CHUNK
    rows = tm + 2 * HALO
    n_rt = rows // SUBLANES
    n_slab = d // LANES
    j = pl.program_id(0) % tiles_per_seq
    keep_prev = (j != 0).astype(F32)
    keep_next = (j != tiles_per_seq - 1).astype(F32)
    g = g_ref[...]
    for lo, hi, h in ((0, HALO, _rms(xp_ref[...], g) * keep_prev), (HALO, HALO + tm, _rms(x_ref[...], g)),
                      (HALO + tm, rows, _rms(xn_ref[...], g) * keep_next)):
        for k in range(n_slab):
            hs_ref[k, lo:hi, :] = h[:, k * LANES:(k + 1) * LANES]
    hperm = jnp.concatenate(
        [jnp.concatenate([hs_ref[k, pl.ds(i, SUBLANES, stride=n_rt), :] for k in range(n_slab)], axis=1)
         for i in range(n_rt)], axis=0).astype(BF16)
    for c in range(n_chunks):
        z = _dot(hperm, wup_ref[c])
        cw = cw_ref[c]
        zp = jnp.concatenate([pltpu.roll(z[rows - SUBLANES:], 1, 0), z[:rows - SUBLANES]], axis=0)
        zn = jnp.concatenate([z[SUBLANES:], pltpu.roll(z[:SUBLANES], SUBLANES - 1, 0)], axis=0)
        zz = zp * cw[0:1] + z * cw[1:2] + zn * cw[2:3] + cb_ref[c]
        gate, val = zz[:, :ck], zz[:, ck:]
        act_ref[:, c * ck:(c + 1) * ck] = (gate * jax.nn.sigmoid(gate) * val).astype(BF16)
    yp = _dot(act_ref[...], wdn_ref[...])
    for i in range(n_rt):
        for k in range(n_slab):
            ys_ref[k, pl.ds(i, SUBLANES, stride=n_rt), :] = yp[i * SUBLANES:(i + 1) * SUBLANES,
                                                               k * LANES:(k + 1) * LANES]
    y = x_ref[...] + jnp.concatenate([ys_ref[k, HALO:HALO + tm, :] for k in range(n_slab)], axis=1)
    if final_norm:
        y = _rms(y, gf_ref[...])
    o_ref[...] = y


def _ffn(x, ln, w_up, conv_w, conv_b, w_down, ln_final, seq, final_norm):
    t, d = x.shape
    f = w_down.shape[0]
    n_chunks = f // FF_CHUNK
    n_tiles = t // TM

    def pair(a):
        r = a.shape[0]
        return a.reshape(r, 2, n_chunks, FF_CHUNK).transpose(2, 0, 1, 3).reshape(n_chunks, r, 2 * FF_CHUNK)

    wup = pair(w_up.astype(BF16))
    cw = pair(conv_w)
    cb = pair(conv_b.reshape(1, -1))
    prev, main, nxt = _halo_specs(TM, d, n_tiles)
    kern = functools.partial(_ffn_kernel, tiles_per_seq=seq // TM, n_chunks=n_chunks, final_norm=final_norm)
    return pl.pallas_call(
        kern,
        grid=(n_tiles,),
        in_specs=[prev, main, nxt, _resident((1, d)), _resident(wup.shape), _resident(cw.shape),
                  _resident(cb.shape), _resident(w_down.shape), _resident((1, d))],
        out_specs=pl.BlockSpec((TM, d), lambda i: (i, 0)),
        out_shape=jax.ShapeDtypeStruct((t, d), F32),
        scratch_shapes=[pltpu.VMEM((d // LANES, TM + 2 * HALO, LANES), F32),
                        pltpu.VMEM((d // LANES, TM + 2 * HALO, LANES), F32),
                        pltpu.VMEM((TM + 2 * HALO, f), BF16)],
        compiler_params=_params(),
        name="ffn",
    )(x, x, x, ln.reshape(1, d), wup, cw, cb, w_down.astype(BF16), ln_final.reshape(1, d))


def _mixer_a_kernel(x_ref, g_ref, win_ref, bin_ref, vn_ref, ws_ref, bs_ref, wout_ref, o_ref, sv_ref):
    tm = x_ref.shape[0]
    width = wout_ref.shape[0]
    gw = width // A_GROUPS
    x = x_ref[...]
    h = _rms(x, g_ref[...]).astype(BF16)
    z = _dot(h, win_ref[...]) + bin_ref[...]
    z = 0.5 * z * (1.0 + lax.erf(z * (1.0 / math.sqrt(2.0))))
    u = z[:, :width]
    v = _rms(z[:, width:], vn_ref[...]).astype(BF16)
    for n in range(tm // A_CHUNK):
        r0 = n * A_CHUNK
        for a in range(A_GROUPS):
            c0 = a * gw
            sv_ref[r0:r0 + A_CHUNK, c0:c0 + gw] = _dot(ws_ref[a], v[r0:r0 + A_CHUNK, c0:c0 + gw]) + bs_ref[a]
    o_ref[...] = x + _dot((u * sv_ref[...]).astype(BF16), wout_ref[...])


def _mixer_a(x, ln, w_in, b_in, v_norm, w_s, b_s, w_out):
    t, d = x.shape
    width = w_out.shape[0]
    gw = width // A_GROUPS
    bsb = jnp.broadcast_to(b_s[:, :, None], (A_GROUPS, A_CHUNK, gw))
    return pl.pallas_call(
        _mixer_a_kernel,
        grid=(t // TM,),
        in_specs=[pl.BlockSpec((TM, d), lambda i: (i, 0)), _resident((1, d)), _resident(w_in.shape),
                  _resident((1, 2 * width)), _resident((1, width)), _resident(w_s.shape), _resident(bsb.shape),
                  _resident(w_out.shape)],
        out_specs=pl.BlockSpec((TM, d), lambda i: (i, 0)),
        out_shape=jax.ShapeDtypeStruct((t, d), F32),
        scratch_shapes=[pltpu.VMEM((TM, width), F32)],
        compiler_params=_params(),
        name="mixer_a",
    )(x, ln.reshape(1, d), w_in.astype(BF16), b_in.reshape(1, -1), v_norm.reshape(1, -1), w_s.astype(BF16), bsb,
      w_out.astype(BF16))


def _mixer_b_kernel(xp_ref, x_ref, xn_ref, g_ref, pool_ref, wg_ref, sc_ref, o_ref, *, tiles_per_seq, seq):
    tm, d = x_ref.shape
    gw = d // len(POOL_WINDOWS)
    j = pl.program_id(0) % tiles_per_seq
    keep_prev = (j != 0).astype(F32)
    keep_next = (j != tiles_per_seq - 1).astype(F32)
    g = g_ref[...]
    x = x_ref[...]
    h = _rms(x, g)
    hext = jnp.concatenate([_rms(xp_ref[...], g) * keep_prev, h, _rms(xn_ref[...], g) * keep_next], axis=0)
    hi = hext.astype(BF16)
    lo = (hext - hi.astype(F32)).astype(BF16)
    pos = j * tm + lax.broadcasted_iota(jnp.int32, (tm, 1), 0)
    sub, span = pool_ref.shape[1], pool_ref.shape[2]
    for a, w in enumerate(POOL_WINDOWS):
        c0 = a * gw
        band = pool_ref[a]
        wsum = jnp.concatenate([_dot(band, hi[r0:r0 + span, c0:c0 + gw]) + _dot(band, lo[r0:r0 + span, c0:c0 + gw])
                                for r0 in range(0, tm, sub)], axis=0)
        cnt = (jnp.minimum(pos + w // 2, seq) - jnp.maximum(pos - w // 2, 0)).astype(F32)
        pooled = (wsum / cnt - h[:, c0:c0 + gw]).astype(BF16)
        o_ref[:, c0:c0 + gw] = x[:, c0:c0 + gw] + _dot(pooled, wg_ref[a]) * sc_ref[:, c0:c0 + gw]


def _pool_bands(tm):
    t = np.arange(tm)[:, None] + HALO
    jj = np.arange(tm + 2 * HALO)[None, :]
    return np.stack([((jj >= t - w // 2) & (jj < t + w // 2)) for w in POOL_WINDOWS]).astype(np.float32)


def _mixer_b(x, ln, w_grp, scale, seq):
    t, d = x.shape
    n_tiles = t // TM
    bands = jnp.asarray(_pool_bands(POOL_SUB), BF16)
    prev, main, nxt = _halo_specs(TM, d, n_tiles)
    kern = functools.partial(_mixer_b_kernel, tiles_per_seq=seq // TM, seq=seq)
    return pl.pallas_call(
        kern,
        grid=(n_tiles,),
        in_specs=[prev, main, nxt, _resident((1, d)), _resident(bands.shape), _resident(w_grp.shape),
                  _resident((1, d))],
        out_specs=pl.BlockSpec((TM, d), lambda i: (i, 0)),
        out_shape=jax.ShapeDtypeStruct((t, d), F32),
        compiler_params=_params(),
        name="mixer_b",
    )(x, x, x, ln.reshape(1, d), bands, w_grp.astype(BF16), scale.reshape(1, d))


def _rope(r, cos, sin_lo, sin_hi):
    half = C_ROPE // 2
    return r * cos + pltpu.roll(r, LANES - half, 1) * sin_lo + pltpu.roll(r, half, 1) * sin_hi


def _mla_proj_kernel(x_ref, g_ref, wdq_ref, qn_ref, wuqt_ref, wdkv_ref, kvn_ref, wuk_ref, wvt_ref, cos_ref, slo_ref,
                     shi_ref, cost_ref, sint_ref, qt_ref, k_ref, vt_ref, *, q_scale):
    nt = (((1,), (1,)), ((), ()))
    half = C_ROPE // 2
    h = _rms(x_ref[...], g_ref[...]).astype(BF16)
    cq = _rms(_dot(h, wdq_ref[...]), qn_ref[...]).astype(BF16)
    qt = lax.dot_general(wuqt_ref[...], cq, nt, preferred_element_type=F32)
    kv = _dot(h, wdkv_ref[...])
    ckv = _rms(kv[:, :C_KV_LORA], kvn_ref[...]).astype(BF16)
    k_rope = _rope(kv[:, C_KV_LORA:], cos_ref[...], slo_ref[...], shi_ref[...]).astype(BF16)
    k_nope = _dot(ckv, wuk_ref[...])
    cos_t, sin_t = cost_ref[...], sint_ref[...]

    def put_q(lo, hi, val):
        val = val.astype(BF16)
        for u in range(qt_ref.shape[0]):
            qt_ref[u, lo:hi, :] = val[:, u * TQ:(u + 1) * TQ]

    for a in range(C_HEADS):
        q0 = a * HEAD_PAD
        r0 = q0 + C_NOPE
        x1, x2 = qt[r0:r0 + half], qt[r0 + half:r0 + C_ROPE]
        put_q(q0, r0, qt[q0:r0] * q_scale)
        put_q(r0, r0 + half, (x1 * cos_t - x2 * sin_t) * q_scale)
        put_q(r0 + half, r0 + C_ROPE, (x1 * sin_t + x2 * cos_t) * q_scale)
        put_q(r0 + C_ROPE, q0 + HEAD_PAD, qt[r0 + C_ROPE:q0 + HEAD_PAD])
        k_ref[:, q0:q0 + C_NOPE] = k_nope[:, a * C_NOPE:(a + 1) * C_NOPE].astype(BF16)
        k_ref[:, q0 + C_NOPE:q0 + HEAD_PAD] = k_rope
    vt_ref[...] = lax.dot_general(wvt_ref[...], ckv, nt, preferred_element_type=F32).astype(BF16)


def _rope_lane_tables(seq):
    half = C_ROPE // 2
    inv = 1.0 / (ROPE_BASE ** (jnp.arange(0, C_ROPE, 2, dtype=F32) / C_ROPE))
    ang = jnp.arange(seq, dtype=F32)[:, None] * inv[None, :]
    cos, sin = jnp.cos(ang), jnp.sin(ang)
    zero = jnp.zeros((seq, half), F32)
    pad = jnp.zeros((seq, LANES - C_ROPE), F32)
    cos_t = jnp.concatenate([cos, cos, pad], axis=1)
    sin_lo = jnp.concatenate([-sin, zero, pad], axis=1)
    sin_hi = jnp.concatenate([zero, sin, pad], axis=1)
    return cos_t, sin_lo, sin_hi, cos.T, sin.T


def _mla_proj(x, ln, w_dq, q_norm, w_uq, w_dkv, kv_norm, w_ukv, seq):
    t, d = x.shape
    q_lora = w_dq.shape[1]
    wuq = w_uq.reshape(q_lora, C_HEADS, C_QK)
    wuqt = jnp.pad(wuq, ((0, 0), (0, 0), (0, HEAD_PAD - C_QK))).reshape(q_lora, C_HEADS * HEAD_PAD).T.astype(BF16)
    wdkv = jnp.pad(w_dkv, ((0, 0), (0, LANES - C_ROPE))).astype(BF16)
    wukv = w_ukv.reshape(C_KV_LORA, C_HEADS, C_NOPE + C_V)
    wuk = wukv[:, :, :C_NOPE].reshape(C_KV_LORA, -1).astype(BF16)
    wvt = wukv[:, :, C_NOPE:].reshape(C_KV_LORA, -1).T.astype(BF16)
    cos_l, sin_lo, sin_hi, cos_t, sin_t = _rope_lane_tables(seq)
    tps = seq // TM
    tab = pl.BlockSpec((TM, LANES), lambda i: (i % tps, 0))
    tab_t = pl.BlockSpec((C_ROPE // 2, TM), lambda i: (0, i % tps))
    feat = lambda n: pl.BlockSpec((None, n, TM), lambda i: (i // tps, 0, i % tps))
    tok = lambda w: pl.BlockSpec((TM, w), lambda i: (i, 0))
    kern = functools.partial(_mla_proj_kernel, q_scale=(C_QK ** -0.5) * math.log2(math.e))
    return pl.pallas_call(
        kern,
        grid=(t // TM,),
        in_specs=[tok(d), _resident((1, d)), _resident(w_dq.shape), _resident((1, q_lora)), _resident(wuqt.shape),
                  _resident(wdkv.shape), _resident((1, C_KV_LORA)), _resident(wuk.shape), _resident(wvt.shape),
                  tab, tab, tab, tab_t, tab_t],
        out_specs=[pl.BlockSpec((None, TM // TQ, C_HEADS * HEAD_PAD, TQ), lambda i: (i // tps, i % tps, 0, 0)),
                   tok(C_HEADS * HEAD_PAD), feat(C_HEADS * C_V)],
        out_shape=[jax.ShapeDtypeStruct((t // seq, seq // TQ, C_HEADS * HEAD_PAD, TQ), BF16),
                   jax.ShapeDtypeStruct((t, C_HEADS * HEAD_PAD), BF16),
                   jax.ShapeDtypeStruct((t // seq, C_HEADS * C_V, seq), BF16)],
        compiler_params=_params(),
        name="mla_proj",
    )(x, ln.reshape(1, d), w_dq.astype(BF16), q_norm.reshape(1, -1), wuqt, wdkv, kv_norm.reshape(1, -1), wuk, wvt,
      cos_l, sin_lo, sin_hi, cos_t, sin_t)


def _col_reduce(x, op, n_acc=2):
    tiles = [x[r:r + SUBLANES] for r in range(0, x.shape[0], SUBLANES)]
    accs = tiles[:n_acc]
    for i, tile in enumerate(tiles[n_acc:]):
        accs[i % n_acc] = op(accs[i % n_acc], tile)
    while len(accs) > 1:
        accs = [op(accs[i], accs[i + 1]) for i in range(0, len(accs), 2)]
    if op is jnp.add:
        return jnp.sum(accs[0], axis=0, keepdims=True)
    return jnp.max(accs[0], axis=0, keepdims=True)


def _attn_kernel(qt_ref, k_ref, vt_ref, o_ref, s_ref, smax_ref, m_ref, l_ref, acc_ref, *, n_kv, n_q):
    heads = range(HEADS_PER_STEP)

    def scores(qi, j, a):
        r0 = pl.multiple_of(j * TK, TK)
        k = k_ref[pl.ds(r0, TK), a * HEAD_PAD:(a + 1) * HEAD_PAD]
        return _dot(k, qt_ref[qi, a * HEAD_PAD:(a + 1) * HEAD_PAD, :])

    def consume(j, a):
        r0 = pl.multiple_of(j * TK, TK)
        m_old = m_ref[a]
        m_new = jnp.maximum(m_old, smax_ref[a])
        p = jnp.exp2(s_ref[a] - m_new)
        alpha = jnp.exp2(m_old - m_new)
        l_ref[a] = alpha * l_ref[a] + _col_reduce(p, jnp.add)
        pv = _dot(vt_ref[a * C_V:(a + 1) * C_V, pl.ds(r0, TK)], p.astype(BF16))
        acc_ref[a] = alpha * acc_ref[a] + pv
        m_ref[a] = m_new

    def advance(j, qi_next, j_next):
        nxt = [scores(qi_next, j_next, a) for a in heads]
        nxt_max = [_col_reduce(x, jnp.maximum) for x in nxt]
        for a in heads:
            consume(j, a)
        return nxt, nxt_max

    def publish(nxt, nxt_max):
        for a in heads:
            s_ref[a] = nxt[a]
            smax_ref[a] = nxt_max[a]

    first = [scores(0, 0, a) for a in heads]
    publish(first, [_col_reduce(x, jnp.maximum) for x in first])

    def q_tile(qi, carry):
        m_ref[...] = jnp.full(m_ref.shape, -jnp.inf, F32)
        l_ref[...] = jnp.zeros(l_ref.shape, F32)
        acc_ref[...] = jnp.zeros(acc_ref.shape, F32)

        def step(j, c):
            publish(*advance(j, qi, j + 1))
            return c

        lax.fori_loop(0, n_kv - 1, step, 0, unroll=ATTN_UNROLL if (n_kv - 1) > ATTN_UNROLL else 1)
        nxt = advance(n_kv - 1, jnp.minimum(qi + 1, n_q - 1), 0)
        q0 = pl.multiple_of(qi * TQ, TQ)
        for a in heads:
            o_ref[pl.ds(q0, TQ), a * C_V:(a + 1) * C_V] = (acc_ref[a] / l_ref[a]).T.astype(o_ref.dtype)
        publish(*nxt)
        return carry

    lax.fori_loop(0, n_q, q_tile, 0)


def _attention(qt, k, vt, batch, seq):
    t = k.shape[0]
    nq = seq // TQ
    n_q = min(nq, Q_PER_STEP)
    hs = HEADS_PER_STEP
    kern = functools.partial(_attn_kernel, n_kv=seq // TK, n_q=n_q)
    return pl.pallas_call(
        kern,
        grid=(batch, C_HEADS // hs, nq // n_q),
        in_specs=[pl.BlockSpec((None, n_q, hs * HEAD_PAD, TQ), lambda b, a, i: (b, i, a, 0)),
                  pl.BlockSpec((seq, hs * HEAD_PAD), lambda b, a, i: (b, a)),
                  pl.BlockSpec((None, hs * C_V, seq), lambda b, a, i: (b, a, 0))],
        out_specs=pl.BlockSpec((n_q * TQ, hs * C_V), lambda b, a, i: (b * (nq // n_q) + i, a)),
        out_shape=jax.ShapeDtypeStruct((t, C_HEADS * C_V), BF16),
        scratch_shapes=[pltpu.VMEM((hs, TK, TQ), F32), pltpu.VMEM((hs, 1, TQ), F32), pltpu.VMEM((hs, 1, TQ), F32),
                        pltpu.VMEM((hs, 1, TQ), F32), pltpu.VMEM((hs, C_V, TQ), F32)],
        compiler_params=_params(3),
        name="mla_attn",
    )(qt, k, vt)


def _out_proj_kernel(x_ref, o_ref, w_ref, y_ref):
    y_ref[...] = x_ref[...] + _dot(o_ref[...], w_ref[...])


def _out_proj(x, o, w_o):
    t, d = x.shape
    return pl.pallas_call(
        _out_proj_kernel,
        grid=(t // TM,),
        in_specs=[pl.BlockSpec((TM, d), lambda i: (i, 0)), pl.BlockSpec((TM, o.shape[1]), lambda i: (i, 0)),
                  _resident(w_o.shape)],
        out_specs=pl.BlockSpec((TM, d), lambda i: (i, 0)),
        out_shape=jax.ShapeDtypeStruct((t, d), F32),
        compiler_params=_params(),
        name="mla_out",
    )(x, o, w_o.astype(BF16))


def _trunk(x3, p):
    batch, seq, d = x3.shape
    assert seq % TM == 0 and TM % TQ == 0 and seq % TK == 0 and TM % A_CHUNK == 0
    assert (seq // TQ) % min(seq // TQ, Q_PER_STEP) == 0
    x = x3.reshape(batch * seq, d)
    depth = p["ln_mix"].shape[0]
    for i in range(depth):
        j, kind = divmod(i, 3)
        if kind == 0:
            x = _mixer_a(x, p["ln_mix"][i], p["a_w_in"][j], p["a_b_in"][j], p["a_v_norm"][j], p["a_w_s"][j],
                         p["a_b_s"][j], p["a_w_out"][j])
        elif kind == 1:
            x = _mixer_b(x, p["ln_mix"][i], p["b_w_grp"][j], p["b_scale"][j], seq)
        else:
            q, k, v = _mla_proj(x, p["ln_mix"][i], p["c_w_dq"][j], p["c_q_norm"][j], p["c_w_uq"][j],
                                p["c_w_dkv"][j], p["c_kv_norm"][j], p["c_w_ukv"][j], seq)
            x = _out_proj(x, _attention(q, k, v, batch, seq), p["c_w_o"][j])
        x = _ffn(x, p["ln_ffn"][i], p["f_w_up"][i], p["f_conv_w"][i], p["f_conv_b"][i], p["f_w_down"][i],
                 p["ln_final"], seq, final_norm=(i == depth - 1))
    return x.reshape(batch, seq, d)


def kernel(x_prompt, x_sample, ln_mix, ln_ffn, ln_final, a_w_in, a_b_in, a_v_norm, a_w_s, a_b_s, a_w_out, b_w_grp,
           b_scale, c_w_dq, c_q_norm, c_w_uq, c_w_dkv, c_kv_norm, c_w_ukv, c_w_o, f_w_up, f_conv_w, f_conv_b,
           f_w_down):
    p = dict(ln_mix=ln_mix, ln_ffn=ln_ffn, ln_final=ln_final, a_w_in=a_w_in, a_b_in=a_b_in, a_v_norm=a_v_norm,
             a_w_s=a_w_s, a_b_s=a_b_s, a_w_out=a_w_out, b_w_grp=b_w_grp, b_scale=b_scale, c_w_dq=c_w_dq,
             c_q_norm=c_q_norm, c_w_uq=c_w_uq, c_w_dkv=c_w_dkv, c_kv_norm=c_kv_norm, c_w_ukv=c_w_ukv, c_w_o=c_w_o,
             f_w_up=f_w_up, f_conv_w=f_conv_w, f_conv_b=f_conv_b, f_w_down=f_w_down)
    return (_trunk(x_prompt, p), _trunk(x_sample, p))
```
